```python
import math
import jax
import jax.numpy as jnp
from jax import lax
import numpy as np

D_MODEL = 1024
BATCH = 16
SEQ = 2048
DEPTH = 1

N_META = 16
N_HEADS = 16
N_KV_HEADS = 4
HEAD_DIM = 64
Q_PER_KV = N_HEADS // N_KV_HEADS
Q_DIM = N_HEADS * HEAD_DIM
KV_DIM = N_KV_HEADS * HEAD_DIM
WINDOW = 128
ATT_BLOCK = 128
N_BUCKETS = 32
MAX_DISTANCE = 128
SSM_EXPAND = 2
D_INNER = SSM_EXPAND * D_MODEL
SSM_HEAD_DIM = 64
N_SSM_HEADS = D_INNER // SSM_HEAD_DIM
N_GROUPS = 4
HEADS_PER_GROUP = N_SSM_HEADS // N_GROUPS
D_STATE = 128
CONV_WIDTH = 4
CONV_DIM = D_INNER + 2 * N_GROUPS * D_STATE
CHUNK = 128
DT_MIN = 0.001
DT_MAX = 0.1
N_EXPERTS = 32
TOP_K = 4
D_FF = D_MODEL
SWIGLU_ALPHA = 1.702
SWIGLU_LIMIT = 7.0
MOE_BLOCK = 256
N_BRANCHES = 2
EPS = 1e-5
IN_SIZES = (Q_DIM, KV_DIM, KV_DIM, D_INNER, CONV_DIM, N_SSM_HEADS, N_BRANCHES * D_MODEL)
IN_DIM = sum(IN_SIZES)

kernel_name = "hybrid_swa_ssd_moe_block"


def rms_norm(x, g):
    xf = x.astype(jnp.float32)
    y = xf * lax.rsqrt(jnp.mean(xf * xf, axis=-1, keepdims=True) + EPS)
    return (y * g.astype(jnp.float32)).astype(x.dtype)


def t5_bucket(dist):
    n = jnp.maximum(dist, 0)
    max_exact = N_BUCKETS // 2
    nf = jnp.maximum(n, 1).astype(jnp.float32)
    large = max_exact + (jnp.log(nf / max_exact) / math.log(MAX_DISTANCE / max_exact)
                         * (N_BUCKETS - max_exact)).astype(jnp.int32)
    large = jnp.minimum(large, N_BUCKETS - 1)
    return jnp.where(n < max_exact, n, large)


def sink_softmax(logits, sinks):
    sink = jnp.broadcast_to(sinks, logits.shape[:-1] + (1,))
    p = jax.nn.softmax(jnp.concatenate([logits, sink], axis=-1), axis=-1)
    return p[..., :-1]


def sliding_window_attention(q, k, v, sinks, rel_bias):
    Bsz, L = q.shape[0], q.shape[1]
    S = L - N_META
    nb = S // ATT_BLOCK
    ctx = N_META + 2 * ATT_BLOCK
    scale = HEAD_DIM ** -0.5
    sinks_f = sinks.astype(jnp.float32).reshape(N_KV_HEADS, Q_PER_KV)[:, :, None, None]
    table = rel_bias.astype(jnp.float32)

    qm = q[:, :N_META].reshape(Bsz, N_META, N_KV_HEADS, Q_PER_KV, HEAD_DIM)
    km, vm = k[:, :N_META], v[:, :N_META]
    im = jnp.arange(N_META, dtype=jnp.int32)
    dist_m = im[:, None] - im[None, :]
    bias_m = table[t5_bucket(dist_m)].reshape(N_META, N_META, N_KV_HEADS, Q_PER_KV).transpose(2, 3, 0, 1)
    lm = jnp.einsum('bqkgd,bskd->bkgqs', qm, km).astype(jnp.float32) * scale + bias_m
    lm = jnp.where(dist_m >= 0, lm, -jnp.inf)
    pm = sink_softmax(lm, sinks_f)
    om = jnp.einsum('bkgqs,bskd->bqkgd', pm.astype(v.dtype), vm).reshape(Bsz, N_META, Q_DIM)

    qr = q[:, N_META:].reshape(Bsz, nb, ATT_BLOCK, N_KV_HEADS, Q_PER_KV, HEAD_DIM)

    def with_context(t, t_meta):
        t = t.reshape(Bsz, nb, ATT_BLOCK, N_KV_HEADS, HEAD_DIM)
        prev = jnp.concatenate([jnp.zeros_like(t[:, :1]), t[:, :-1]], axis=1)
        meta = jnp.broadcast_to(t_meta[:, None], (Bsz, nb, N_META, N_KV_HEADS, HEAD_DIM))
        return jnp.concatenate([meta, prev, t], axis=2)

    kc = with_context(k[:, N_META:], km)
    vc = with_context(v[:, N_META:], vm)

    n_idx = jnp.arange(nb, dtype=jnp.int32)[:, None]
    j = jnp.arange(ATT_BLOCK, dtype=jnp.int32)[None, :]
    kpos = jnp.concatenate([
        jnp.broadcast_to(jnp.arange(N_META, dtype=jnp.int32)[None, :], (nb, N_META)),
        N_META + (n_idx - 1) * ATT_BLOCK + j,
        N_META + n_idx * ATT_BLOCK + j], axis=1)
    qpos = N_META + n_idx * ATT_BLOCK + j
    dist = qpos[:, :, None] - kpos[:, None, :]
    is_meta_slot = (jnp.arange(ctx) < N_META)[None, None, :]
    valid = is_meta_slot | ((dist >= 0) & (dist < WINDOW) & (kpos[:, None, :] >= N_META))
    bias = table[t5_bucket(dist)].reshape(nb, ATT_BLOCK, ctx, N_KV_HEADS, Q_PER_KV).transpose(0, 3, 4, 1, 2)

    logits = jnp.einsum('bnqkgd,bnskd->bnkgqs', qr, kc).astype(jnp.float32) * scale + bias[None]
    logits = jnp.where(valid[None, :, None, None], logits, -jnp.inf)
    p = sink_softmax(logits, sinks_f)
    o = jnp.einsum('bnkgqs,bnskd->bnqkgd', p.astype(v.dtype), vc).reshape(Bsz, S, Q_DIM)
    return jnp.concatenate([om, o], axis=1)


def causal_depthwise_conv(u, w, b):
    y = lax.conv_general_dilated(
        u, w[:, None, :].astype(u.dtype), window_strides=(1,), padding=[(CONV_WIDTH - 1, 0)],
        dimension_numbers=('NWC', 'WIO', 'NWC'), feature_group_count=u.shape[-1])
    return y + b.astype(u.dtype)


def ssd_scan(x, dt, A, Bm, Cm):
    Bsz, Lp = x.shape[0], x.shape[1]
    nc = Lp // CHUNK
    xc = x.reshape(Bsz, nc, CHUNK, N_GROUPS, HEADS_PER_GROUP, SSM_HEAD_DIM)
    dtc = dt.reshape(Bsz, nc, CHUNK, N_GROUPS, HEADS_PER_GROUP)
    Bc = Bm.reshape(Bsz, nc, CHUNK, N_GROUPS, D_STATE)
    Cc = Cm.reshape(Bsz, nc, CHUNK, N_GROUPS, D_STATE)
    a = jnp.moveaxis(dtc * A.reshape(N_GROUPS, HEADS_PER_GROUP), 2, -1)
    a_cs = jnp.cumsum(a, axis=-1)
    xdt = xc * dtc[..., None]

    causal = jnp.tril(jnp.ones((CHUNK, CHUNK), dtype=bool))
    decay = jnp.exp(jnp.where(causal, a_cs[..., :, None] - a_cs[..., None, :], -jnp.inf))
    cb = jnp.einsum('bclgn,bcsgn->bcgls', Cc, Bc)
    y_diag = jnp.einsum('bcgls,bcgrls,bcsgrp->bclgrp', cb, decay, xdt)

    decay_to_end = jnp.exp(a_cs[..., -1:] - a_cs)
    states = jnp.einsum('bclgn,bcgrl,bclgrp->bcgrpn', Bc, decay_to_end, xdt)
    chunk_decay = jnp.exp(a_cs[..., -1])

    def step(h, inp):
        s_c, d_c = inp
        return h * d_c[..., None, None] + s_c, h

    h0 = jnp.zeros((Bsz, N_GROUPS, HEADS_PER_GROUP, SSM_HEAD_DIM, D_STATE), jnp.float32)
    _, h_in = lax.scan(step, h0, (jnp.moveaxis(states, 1, 0), jnp.moveaxis(chunk_decay, 1, 0)))
    h_in = jnp.moveaxis(h_in, 0, 1)
    y_off = jnp.einsum('bclgn,bcgrpn,bcgrl->bclgrp', Cc, h_in, jnp.exp(a_cs))
    return (y_diag + y_off).reshape(Bsz, Lp, N_SSM_HEADS, SSM_HEAD_DIM)


def ssd_mixer(z, xbc, dt_raw, conv_w, conv_b, dt_bias, a_log, d_skip, norm_g):
    Bsz, L = z.shape[0], z.shape[1]
    xbc = jax.nn.silu(causal_depthwise_conv(xbc, conv_w, conv_b)).astype(jnp.float32)
    xs = xbc[..., :D_INNER].reshape(Bsz, L, N_SSM_HEADS, SSM_HEAD_DIM)
    Bm = xbc[..., D_INNER:D_INNER + N_GROUPS * D_STATE].reshape(Bsz, L, N_GROUPS, D_STATE)
    Cm = xbc[..., D_INNER + N_GROUPS * D_STATE:].reshape(Bsz, L, N_GROUPS, D_STATE)
    dt = jax.nn.softplus(dt_raw.astype(jnp.float32) + dt_bias.astype(jnp.float32))
    A = -jnp.exp(a_log.astype(jnp.float32))
    pad = CHUNK - N_META

    def padl(t):
        return jnp.pad(t, [(0, 0), (pad, 0)] + [(0, 0)] * (t.ndim - 2))

    y = ssd_scan(padl(xs), padl(dt), A, padl(Bm), padl(Cm))[:, pad:]
    y = y + d_skip.astype(jnp.float32)[:, None] * xs
    y = y.reshape(Bsz, L, D_INNER) * jax.nn.silu(z.astype(jnp.float32))
    yg = y.reshape(Bsz, L, N_GROUPS, D_INNER // N_GROUPS)
    yg = yg * lax.rsqrt(jnp.mean(yg * yg, axis=-1, keepdims=True) + EPS)
    y = yg.reshape(Bsz, L, D_INNER) * norm_g.astype(jnp.float32)
    return y.astype(z.dtype)


def swiglu_clamped(hh):
    x_glu = jnp.minimum(hh[..., ::2], SWIGLU_LIMIT)
    x_lin = jnp.clip(hh[..., 1::2], -SWIGLU_LIMIT, SWIGLU_LIMIT)
    return x_glu * jax.nn.sigmoid(SWIGLU_ALPHA * x_glu) * (x_lin + 1.0)


def moe_ffn(u, w_router, b_router, w1, b1, w2, b2):
    T = u.shape[0]
    logits = (u @ w_router + b_router).astype(jnp.float32)
    top_vals, top_idx = lax.top_k(logits, TOP_K)
    top_w = jax.nn.softmax(top_vals, axis=-1).astype(u.dtype)
    n_assign = T * TOP_K
    flat_e = top_idx.reshape(-1)
    flat_tok = jnp.repeat(jnp.arange(T, dtype=jnp.int32), TOP_K)
    flat_w = top_w.reshape(-1)
    order = jnp.argsort(flat_e)
    sorted_e = flat_e[order]
    counts = jnp.bincount(flat_e, length=N_EXPERTS)
    padded = (counts + MOE_BLOCK - 1) // MOE_BLOCK * MOE_BLOCK
    start = jnp.cumsum(counts) - counts
    pend = jnp.cumsum(padded)
    pstart = pend - padded
    dest = jnp.arange(n_assign, dtype=jnp.int32) + (pstart - start)[sorted_e]
    n_rows = -(-(n_assign + N_EXPERTS * (MOE_BLOCK - 1)) // MOE_BLOCK) * MOE_BLOCK
    n_blocks = n_rows // MOE_BLOCK
    row_tok = jnp.zeros((n_rows,), jnp.int32).at[dest].set(flat_tok[order])
    row_w = jnp.zeros((n_rows,), u.dtype).at[dest].set(flat_w[order])
    block_e = jnp.minimum(
        jnp.searchsorted(pend, jnp.arange(n_blocks, dtype=jnp.int32) * MOE_BLOCK, side='right'),
        N_EXPERTS - 1)

    def body(acc, inp):
        tok, wt, e = inp
        hb = swiglu_clamped(u[tok] @ w1[e] + b1[e])
        yb = (hb @ w2[e] + b2[e]) * wt[:, None]
        return acc.at[tok].add(yb), None

    out, _ = lax.scan(body, jnp.zeros_like(u),
                      (row_tok.reshape(n_blocks, MOE_BLOCK), row_w.reshape(n_blocks, MOE_BLOCK), block_e))
    return out


def setup_inputs(seed: int = 0) -> dict:
    key = jax.random.key(seed)
    ks = jax.random.split(key, 24)

    def nrm(k, shape, scale):
        return jax.random.normal(k, shape, jnp.float32) * scale

    def gain(k, shape):
        return 1.0 + nrm(k, shape, 0.02)

    dt0 = jnp.exp(jax.random.uniform(ks[8], (DEPTH, N_SSM_HEADS), jnp.float32)
                  * (math.log(DT_MAX) - math.log(DT_MIN)) + math.log(DT_MIN))
    return {
        "x": nrm(ks[0], (BATCH, SEQ, D_MODEL), 1.0),
        "meta_tokens": nrm(ks[1], (N_META, D_MODEL), 1.0),
        "rel_bias": nrm(ks[2], (N_BUCKETS, N_HEADS), 0.5),
        "norm_mix_g": gain(ks[3], (DEPTH, D_MODEL)),
        "w_in": nrm(ks[4], (DEPTH, D_MODEL, IN_DIM), D_MODEL ** -0.5),
        "attn_sinks": nrm(ks[5], (DEPTH, N_HEADS), 0.5),
        "conv_w": nrm(ks[6], (DEPTH, CONV_WIDTH, CONV_DIM), CONV_WIDTH ** -0.5),
        "conv_b": nrm(ks[7], (DEPTH, CONV_DIM), 0.01),
        "dt_bias": dt0 + jnp.log(-jnp.expm1(-dt0)),
        "a_log": jnp.log(jax.random.uniform(ks[9], (DEPTH, N_SSM_HEADS), jnp.float32, 1.0, 16.0)),
        "d_skip": gain(ks[10], (DEPTH, N_SSM_HEADS)),
        "ssm_norm_g": gain(ks[11], (DEPTH, D_INNER)),
        "w_attn_br": nrm(ks[12], (DEPTH, Q_DIM, D_MODEL), Q_DIM ** -0.5),
        "w_ssm_br": nrm(ks[13], (DEPTH, D_INNER, D_MODEL), D_INNER ** -0.5),
        "w_out": nrm(ks[14], (DEPTH, D_MODEL, D_MODEL), D_MODEL ** -0.5),
        "norm_ffn_g": gain(ks[15], (DEPTH, D_MODEL)),
        "w_router": nrm(ks[16], (DEPTH, D_MODEL, N_EXPERTS), D_MODEL ** -0.5),
        "b_router": nrm(ks[17], (DEPTH, N_EXPERTS), 0.01),
        "w_exp_in": nrm(ks[18], (DEPTH, N_EXPERTS, D_MODEL, 2 * D_FF), D_MODEL ** -0.5),
        "b_exp_in": nrm(ks[19], (DEPTH, N_EXPERTS, 2 * D_FF), 0.01),
        "w_exp_out": nrm(ks[20], (DEPTH, N_EXPERTS, D_FF, D_MODEL), D_FF ** -0.5),
        "b_exp_out": nrm(ks[21], (DEPTH, N_EXPERTS, D_MODEL), 0.01),
        "norm_final_g": gain(ks[22], (D_MODEL,)),
    }


def reference(x, meta_tokens, rel_bias, norm_mix_g, w_in, attn_sinks, conv_w, conv_b, dt_bias, a_log,
              d_skip, ssm_norm_g, w_attn_br, w_ssm_br, w_out, norm_ffn_g, w_router, b_router,
              w_exp_in, b_exp_in, w_exp_out, b_exp_out, norm_final_g):
    Bsz = x.shape[0]
    meta = jnp.broadcast_to(meta_tokens[None].astype(x.dtype), (Bsz, N_META, D_MODEL))
    h = jnp.concatenate([meta, x], axis=1)
    L = h.shape[1]
    splits = np.cumsum(IN_SIZES)[:-1].tolist()
    for layer in range(DEPTH):
        u = rms_norm(h, norm_mix_g[layer])
        q, k, v, z, xbc, dt_raw, gate_logits = jnp.split(u @ w_in[layer], splits, axis=-1)
        y_attn = sliding_window_attention(
            q.reshape(Bsz, L, N_HEADS, HEAD_DIM), k.reshape(Bsz, L, N_KV_HEADS, HEAD_DIM),
            v.reshape(Bsz, L, N_KV_HEADS, HEAD_DIM), attn_sinks[layer], rel_bias)
        y_ssm = ssd_mixer(z, xbc, dt_raw, conv_w[layer], conv_b[layer], dt_bias[layer], a_log[layer],
                          d_skip[layer], ssm_norm_g[layer])
        gates = jax.nn.sigmoid(gate_logits.astype(jnp.float32)).astype(h.dtype).reshape(Bsz, L, N_BRANCHES, D_MODEL)
        merged = gates[:, :, 0] * (y_attn @ w_attn_br[layer]) + gates[:, :, 1] * (y_ssm @ w_ssm_br[layer])
        h = h + merged @ w_out[layer]
        u = rms_norm(h, norm_ffn_g[layer]).reshape(Bsz * L, D_MODEL)
        h = h + moe_ffn(u, w_router[layer], b_router[layer], w_exp_in[layer], b_exp_in[layer],
                        w_exp_out[layer], b_exp_out[layer]).reshape(Bsz, L, D_MODEL)
    h = rms_norm(h, norm_final_g)
    return h[:, N_META:]
```

```python
import functools
import math

import jax
import jax.numpy as jnp
from jax import lax
from jax.experimental import pallas as pl
from jax.experimental.pallas import tpu as pltpu

F32 = jnp.float32
BF16 = jnp.bfloat16

D_MODEL = 1024
N_META = 16
N_HEADS = 16
N_KV_HEADS = 4
HEAD_DIM = 64
Q_PER_KV = N_HEADS // N_KV_HEADS
Q_DIM = N_HEADS * HEAD_DIM
KV_DIM = N_KV_HEADS * HEAD_DIM
WINDOW = 128
BLK = 128
N_BUCKETS = 32
MAX_DISTANCE = 128
D_INNER = 2 * D_MODEL
SSM_HEAD_DIM = 64
N_SSM_HEADS = D_INNER // SSM_HEAD_DIM
N_GROUPS = 4
D_STATE = 128
CONV_WIDTH = 4
BC_DIM = N_GROUPS * D_STATE
CONV_DIM = D_INNER + 2 * BC_DIM
N_EXPERTS = 32
TOP_K = 4
D_FF = D_MODEL
SWIGLU_ALPHA = 1.702
SWIGLU_LIMIT = 7.0
EPS = 1e-5
NEG = -1e30

LANES = 128
KV_DUP = 2 * KV_DIM
N_PAIRS = N_HEADS // 2
N_SSM_PAIRS = N_SSM_HEADS // 2
CTX = 3 * BLK
VMEM_LIMIT = 56 * 1024 * 1024

PROJ_SEGS = (("q", Q_DIM), ("k", KV_DUP), ("v", KV_DUP), ("z", D_INNER), ("xbc", CONV_DIM),
             ("gate", 2 * D_MODEL))
PROJ_N = sum(w for _, w in PROJ_SEGS)
PROJ_CHUNK = 512


def _dot(a, b):
    return jnp.dot(a, b, preferred_element_type=F32)


def _dot_nt(a, b):
    return lax.dot_general(a, b, (((1,), (1,)), ((), ())), preferred_element_type=F32)


def _sigmoid(x):
    return 1.0 / (1.0 + jnp.exp(-x))


def _const_spec(shape):
    nd = len(shape)
    return pl.BlockSpec(shape, lambda *_: (0,) * nd, pipeline_mode=pl.Buffered(1))


def _inproj_kernel(x_ref, g_ref, w_ref, wdt_ref, q_ref, k_ref, v_ref, z_ref, xbc_ref, gate_ref,
                   dt_ref):
    x = x_ref[...]
    ms = jnp.mean(x * x, axis=-1, keepdims=True)
    u = (x * lax.rsqrt(ms + EPS) * g_ref[...]).astype(BF16)
    outs = (q_ref, k_ref, v_ref, z_ref, xbc_ref, gate_ref)
    off = 0
    for ref, (_, width) in zip(outs, PROJ_SEGS):
        for c0 in range(0, width, PROJ_CHUNK):
            ref[:, c0:c0 + PROJ_CHUNK] = _dot(
                u, w_ref[:, off + c0:off + c0 + PROJ_CHUNK]).astype(BF16)
        off += width
    dt_ref[...] = _dot(u, wdt_ref[...])


def _inproj(x2d, g, w, wdt, tm):
    rows = x2d.shape[0]
    row = lambda i: (i, 0)
    out_shape = [jax.ShapeDtypeStruct((rows, wd), BF16) for _, wd in PROJ_SEGS]
    out_shape.append(jax.ShapeDtypeStruct((rows, LANES), F32))
    out_specs = [pl.BlockSpec((tm, wd), row) for _, wd in PROJ_SEGS]
    out_specs.append(pl.BlockSpec((tm, LANES), row))
    return pl.pallas_call(
        _inproj_kernel,
        grid=(rows // tm,),
        in_specs=[pl.BlockSpec((tm, D_MODEL), row), _const_spec((1, D_MODEL)),
                  _const_spec((D_MODEL, PROJ_N)), _const_spec((D_MODEL, LANES))],
        out_specs=out_specs,
        out_shape=out_shape,
        compiler_params=pltpu.CompilerParams(dimension_semantics=("parallel",),
                                             vmem_limit_bytes=VMEM_LIMIT),
        name="inproj",
    )(x2d, g, w, wdt)


def _attn_kernel(sink_ref, q_ref, kp_ref, kc_ref, km_ref, vp_ref, vc_ref, vm_ref, bmain_ref,
                 bmeta_ref, o_ref):
    n = pl.program_id(1)
    lane_ctx = lax.broadcasted_iota(jnp.int32, (1, CTX), 1)
    pen = jnp.where(jnp.logical_and(n == 0, lane_ctx < BLK), NEG, 0.0).astype(F32)
    lo = lax.broadcasted_iota(jnp.int32, (1, LANES), 1) < HEAD_DIM
    zero = jnp.zeros((), BF16)
    for g in range(N_KV_HEADS):
        cs = slice(g * LANES, (g + 1) * LANES)
        kk = jnp.concatenate([kp_ref[:, cs], kc_ref[:, cs], km_ref[:, cs]], axis=0)
        vv = jnp.concatenate([vp_ref[:, cs], vc_ref[:, cs], vm_ref[:, cs]], axis=0)
        k_lo = jnp.where(lo, kk, zero)
        k_hi = jnp.where(lo, zero, kk)
        v_bd = jnp.concatenate([jnp.where(lo, vv, zero), jnp.where(lo, zero, vv)], axis=0)
        for j in range(Q_PER_KV // 2):
            pi = g * (Q_PER_KV // 2) + j
            qp = q_ref[:, pi * LANES:(pi + 1) * LANES]
            ps, invs = [], []
            for t, kx in enumerate((k_lo, k_hi)):
                h = 2 * pi + t
                bias = jnp.concatenate([bmain_ref[h], bmeta_ref[0, h]], axis=1) + pen
                s = _dot_nt(qp, kx) + bias
                sink = sink_ref[h]
                m = jnp.maximum(jnp.max(s, axis=-1, keepdims=True), sink)
                p = jnp.exp(s - m)
                den = jnp.sum(p, axis=-1, keepdims=True) + jnp.exp(sink - m)
                ps.append(p.astype(BF16))
                invs.append(1.0 / den)
            o = _dot(jnp.concatenate(ps, axis=1), v_bd)
            o = o * jnp.where(lo, invs[0], invs[1])
            o_ref[:, pi * LANES:(pi + 1) * LANES] = o.astype(BF16)


def _attention(q, k, v, k_meta, v_meta, bias_main, bias_meta, sinks, bsz, nb):
    cur = lambda b, n, *_: (b * nb + n, 0)
    prev = lambda b, n, *_: (jnp.maximum(b * nb + n - 1, 0), 0)
    const2 = lambda b, n, *_: (0, 0)
    grid_spec = pltpu.PrefetchScalarGridSpec(
        num_scalar_prefetch=1,
        grid=(bsz, nb),
        in_specs=[
            pl.BlockSpec((BLK, Q_DIM), cur),
            pl.BlockSpec((BLK, KV_DUP), prev), pl.BlockSpec((BLK, KV_DUP), cur),
            pl.BlockSpec((BLK, KV_DUP), const2),
            pl.BlockSpec((BLK, KV_DUP), prev), pl.BlockSpec((BLK, KV_DUP), cur),
            pl.BlockSpec((BLK, KV_DUP), const2),
            pl.BlockSpec((N_HEADS, BLK, 2 * BLK), lambda b, n, *_: (0, 0, 0)),
            pl.BlockSpec((1, N_HEADS, BLK, LANES), lambda b, n, *_: (n, 0, 0, 0)),
        ],
        out_specs=pl.BlockSpec((BLK, Q_DIM), cur),
    )
    return pl.pallas_call(
        _attn_kernel,
        grid_spec=grid_spec,
        out_shape=jax.ShapeDtypeStruct((bsz * nb * BLK, Q_DIM), BF16),
        compiler_params=pltpu.CompilerParams(dimension_semantics=("parallel", "arbitrary"),
                                             vmem_limit_bytes=VMEM_LIMIT),
        name="swa_attention",
    )(sinks, q, k, k, k_meta, v, v, v_meta, bias_main, bias_meta)


CONV_COLS = 512


def _ssd_kernel(xbc_ref, z_ref, dtraw_ref, tail0_ref, h0_ref, convw_ref, convb_ref, dtb_ref,
                alog_ref, dskip_ref, ng_ref, y_ref, hfin_ref, win_ref, xc_ref, st_ref, *, n_pad):
    c = pl.program_id(1)

    @pl.when(c == 0)
    def _():
        win_ref[0:8, :] = tail0_ref[...]
        st_ref[...] = h0_ref[...]

    row = lax.broadcasted_iota(jnp.int32, (BLK, 1), 0)
    live = row >= n_pad
    win_ref[8:8 + BLK, :] = xbc_ref[...].astype(F32)
    for c0 in range(0, CONV_DIM, CONV_COLS):
        cs = slice(c0, c0 + CONV_COLS)
        acc = convb_ref[:, cs] + convw_ref[CONV_WIDTH - 1:CONV_WIDTH, cs] * win_ref[8:8 + BLK, cs]
        for w in range(CONV_WIDTH - 1):
            s0 = 8 - (CONV_WIDTH - 1) + w
            acc = acc + convw_ref[w:w + 1, cs] * win_ref[s0:s0 + BLK, cs]
        act = acc * _sigmoid(acc)
        if n_pad:
            act = jnp.where(live, act, 0.0)
        xc_ref[:, cs] = act
    win_ref[0:8, :] = win_ref[BLK:BLK + 8, :]

    x = dtraw_ref[...] + dtb_ref[...]
    dt = jnp.maximum(x, 0.0) + jnp.log1p(jnp.exp(-jnp.abs(x)))
    if n_pad:
        dt = jnp.where(live, dt, 0.0)
    a = dt * (-jnp.exp(alog_ref[...]))
    ri = lax.broadcasted_iota(jnp.int32, (BLK, BLK), 0)
    ci = lax.broadcasted_iota(jnp.int32, (BLK, BLK), 1)
    causal = ri >= ci
    tril = causal.astype(F32)
    triu = (ri <= ci).astype(F32)
    hi = lax.Precision.HIGHEST
    a_cs = jnp.dot(tril, a, precision=hi, preferred_element_type=F32)
    a_t = a.T[:N_SSM_HEADS]
    dt_t = dt.T[:N_SSM_HEADS]
    a_cs_t = jnp.dot(a_t, triu, precision=hi, preferred_element_type=F32)
    w_t = dt_t * jnp.exp(a_cs_t[:, BLK - 1:BLK] - a_cs_t)
    e_cs = jnp.exp(a_cs)
    lo = lax.broadcasted_iota(jnp.int32, (1, LANES), 1) < SSM_HEAD_DIM

    pairs_per_group = N_SSM_PAIRS // N_GROUPS
    for g in range(N_GROUPS):
        b_g = xc_ref[:, D_INNER + g * D_STATE:D_INNER + (g + 1) * D_STATE]
        c_g = xc_ref[:, D_INNER + BC_DIM + g * D_STATE:D_INNER + BC_DIM + (g + 1) * D_STATE]
        cb = _dot_nt(c_g.astype(BF16), b_g.astype(BF16))
        b_gt = b_g.T
        ys = []
        for j in range(pairs_per_group):
            pi = g * pairs_per_group + j
            ps = slice(pi * LANES, (pi + 1) * LANES)
            xs = xc_ref[:, ps]
            xs_b = xs.astype(BF16)
            st = st_ref[pi]
            rhs = jnp.concatenate([xs_b, st.astype(BF16)], axis=0)
            y_h, st_h = [], []
            for t in range(2):
                h = 2 * pi + t
                diff = a_cs[:, h:h + 1] - a_cs_t[h:h + 1, :]
                dec = jnp.exp(jnp.where(causal, diff, -jnp.inf))
                m_h = cb * dec * dt_t[h:h + 1, :]
                ce = c_g * e_cs[:, h:h + 1]
                lhs = jnp.concatenate([m_h, ce], axis=1).astype(BF16)
                y_h.append(_dot(lhs, rhs))
                s_new = _dot((b_gt * w_t[h:h + 1, :]).astype(BF16), xs_b)
                st_h.append(st * e_cs[BLK - 1:BLK, h:h + 1] + s_new)
            st_ref[pi] = jnp.where(lo, st_h[0], st_h[1])
            ys.append(jnp.where(lo, y_h[0], y_h[1]) + dskip_ref[:, ps] * xs)
        gs = slice(g * (D_INNER // N_GROUPS), (g + 1) * (D_INNER // N_GROUPS))
        zg = z_ref[:, gs].astype(F32)
        yg = jnp.concatenate(ys, axis=1) * (zg * _sigmoid(zg))
        ms = jnp.mean(yg * yg, axis=-1, keepdims=True)
        y_ref[:, gs] = (yg * lax.rsqrt(ms + EPS) * ng_ref[:, gs]).astype(BF16)

    @pl.when(c == pl.num_programs(1) - 1)
    def _():
        hfin_ref[0] = st_ref[...]


def _ssd(xbc, z, dtraw, tail0, h0, convw, convb, dtb, alog, dskip, ng, bsz, nc, n_pad):
    cur = lambda b, c: (b * nc + c, 0)
    return pl.pallas_call(
        functools.partial(_ssd_kernel, n_pad=n_pad),
        grid=(bsz, nc),
        in_specs=[
            pl.BlockSpec((BLK, CONV_DIM), cur), pl.BlockSpec((BLK, D_INNER), cur),
            pl.BlockSpec((BLK, LANES), cur),
            _const_spec((8, CONV_DIM)), _const_spec((N_SSM_PAIRS, D_STATE, LANES)),
            _const_spec((CONV_WIDTH, CONV_DIM)), _const_spec((1, CONV_DIM)),
            _const_spec((1, LANES)), _const_spec((1, LANES)), _const_spec((1, D_INNER)),
            _const_spec((1, D_INNER)),
        ],
        out_specs=[pl.BlockSpec((BLK, D_INNER), cur),
                   pl.BlockSpec((1, N_SSM_PAIRS, D_STATE, LANES), lambda b, c: (b, 0, 0, 0))],
        out_shape=[jax.ShapeDtypeStruct((bsz * nc * BLK, D_INNER), BF16),
                   jax.ShapeDtypeStruct((bsz, N_SSM_PAIRS, D_STATE, LANES), F32)],
        scratch_shapes=[pltpu.VMEM((BLK + 8, CONV_DIM), F32), pltpu.VMEM((BLK, CONV_DIM), F32),
                        pltpu.VMEM((N_SSM_PAIRS, D_STATE, LANES), F32)],
        compiler_params=pltpu.CompilerParams(dimension_semantics=("parallel", "arbitrary"),
                                             vmem_limit_bytes=VMEM_LIMIT),
        name="ssd_mixer",
    )(xbc, z, dtraw, tail0, h0, convw, convb, dtb, alog, dskip, ng)


def _merge_kernel(ya_ref, ys_ref, gl_ref, x_ref, wa_ref, ws_ref, wo_ref, g_ref, wr_ref, br_ref,
                  h_ref, u_ref, idx_ref, tw_ref):
    g0 = _sigmoid(gl_ref[:, :D_MODEL].astype(F32))
    g1 = _sigmoid(gl_ref[:, D_MODEL:].astype(F32))
    merged = g0 * _dot(ya_ref[...], wa_ref[...]) + g1 * _dot(ys_ref[...], ws_ref[...])
    h = x_ref[...] + _dot(merged.astype(BF16), wo_ref[...])
    h_ref[...] = h
    ms = jnp.mean(h * h, axis=-1, keepdims=True)
    u = (h * lax.rsqrt(ms + EPS) * g_ref[...]).astype(BF16)
    u_ref[...] = u
    logits = _dot(u, wr_ref[...]) + br_ref[...]
    lane = lax.broadcasted_iota(jnp.int32, logits.shape, 1).astype(F32)
    idx_out = jnp.zeros(logits.shape, F32)
    val_out = jnp.zeros(logits.shape, F32)
    top = None
    den = None
    for k in range(TOP_K):
        m = jnp.max(logits, axis=-1, keepdims=True)
        sel = jnp.min(jnp.where(logits == m, lane, float(LANES)), axis=-1, keepdims=True)
        if k == 0:
            top = m
        e = jnp.exp(m - top)
        den = e if k == 0 else den + e
        idx_out = jnp.where(lane == k, sel, idx_out)
        val_out = jnp.where(lane == k, e, val_out)
        logits = jnp.where(lane == sel, -jnp.inf, logits)
    idx_ref[...] = idx_out.astype(jnp.int32)
    tw_ref[...] = val_out / den


def _merge(ya, ys, gl, x2d, wa, ws, wo, g, wr, br, tm):
    rows = x2d.shape[0]
    row = lambda i: (i, 0)
    return pl.pallas_call(
        _merge_kernel,
        grid=(rows // tm,),
        in_specs=[pl.BlockSpec((tm, Q_DIM), row), pl.BlockSpec((tm, D_INNER), row),
                  pl.BlockSpec((tm, 2 * D_MODEL), row), pl.BlockSpec((tm, D_MODEL), row),
                  _const_spec((Q_DIM, D_MODEL)), _const_spec((D_INNER, D_MODEL)),
                  _const_spec((D_MODEL, D_MODEL)), _const_spec((1, D_MODEL)),
                  _const_spec((D_MODEL, LANES)), _const_spec((1, LANES))],
        out_specs=[pl.BlockSpec((tm, D_MODEL), row), pl.BlockSpec((tm, D_MODEL), row),
                   pl.BlockSpec((tm, LANES), row), pl.BlockSpec((tm, LANES), row)],
        out_shape=[jax.ShapeDtypeStruct((rows, D_MODEL), F32),
                   jax.ShapeDtypeStruct((rows, D_MODEL), BF16),
                   jax.ShapeDtypeStruct((rows, LANES), jnp.int32),
                   jax.ShapeDtypeStruct((rows, LANES), F32)],
        compiler_params=pltpu.CompilerParams(dimension_semantics=("parallel",),
                                             vmem_limit_bytes=VMEM_LIMIT),
        name="merge_router",
    )(ya, ys, gl, x2d, wa, ws, wo, g, wr, br)


def _moe_kernel(te_ref, nv_ref, x_ref, w1_ref, b1_ref, w2_ref, b2_ref, y_ref):
    i = pl.program_id(0)

    @pl.when(i < nv_ref[0])
    def _():
        hh = _dot(x_ref[...], w1_ref[0]) + b1_ref[0]
        glu = jnp.minimum(hh[:, :D_FF], SWIGLU_LIMIT)
        lin = jnp.clip(hh[:, D_FF:], -SWIGLU_LIMIT, SWIGLU_LIMIT)
        act = glu * _sigmoid(SWIGLU_ALPHA * glu) * (lin + 1.0)
        y_ref[...] = (_dot(act.astype(BF16), w2_ref[0]) + b2_ref[0]).astype(BF16)

    @pl.when(i >= nv_ref[0])
    def _():
        y_ref[...] = jnp.zeros_like(y_ref)


def _moe(tile_e, n_valid, xrows, w1, b1, w2, b2, tm):
    n_tiles = xrows.shape[0] // tm
    wsel = lambda i, te, nv: (te[i], 0, 0)
    grid_spec = pltpu.PrefetchScalarGridSpec(
        num_scalar_prefetch=2,
        grid=(n_tiles,),
        in_specs=[pl.BlockSpec((tm, D_MODEL), lambda i, te, nv: (i, 0)),
                  pl.BlockSpec((1, D_MODEL, 2 * D_FF), wsel), pl.BlockSpec((1, 1, 2 * D_FF), wsel),
                  pl.BlockSpec((1, D_FF, D_MODEL), wsel), pl.BlockSpec((1, 1, D_MODEL), wsel)],
        out_specs=pl.BlockSpec((tm, D_MODEL), lambda i, te, nv: (i, 0)),
    )
    return pl.pallas_call(
        _moe_kernel,
        grid_spec=grid_spec,
        out_shape=jax.ShapeDtypeStruct(xrows.shape, BF16),
        compiler_params=pltpu.CompilerParams(dimension_semantics=("arbitrary",),
                                             vmem_limit_bytes=VMEM_LIMIT),
        name="moe_experts",
    )(tile_e, n_valid, xrows, w1, b1, w2, b2)


def _final_kernel(h_ref, yg_ref, tw_ref, g_ref, o_ref):
    h = h_ref[...]
    tw = tw_ref[...]
    for k in range(TOP_K):
        h = h + tw[:, k:k + 1] * yg_ref[:, k * D_MODEL:(k + 1) * D_MODEL].astype(F32)
    ms = jnp.mean(h * h, axis=-1, keepdims=True)
    o_ref[...] = h * lax.rsqrt(ms + EPS) * g_ref[...]


def _final(h, yg, tw, g, tm):
    rows = h.shape[0]
    row = lambda i: (i, 0)
    return pl.pallas_call(
        _final_kernel,
        grid=(rows // tm,),
        in_specs=[pl.BlockSpec((tm, D_MODEL), row), pl.BlockSpec((tm, TOP_K * D_MODEL), row),
                  pl.BlockSpec((tm, LANES), row), _const_spec((1, D_MODEL))],
        out_specs=pl.BlockSpec((tm, D_MODEL), row),
        out_shape=jax.ShapeDtypeStruct((rows, D_MODEL), F32),
        compiler_params=pltpu.CompilerParams(dimension_semantics=("parallel",),
                                             vmem_limit_bytes=VMEM_LIMIT),
        name="combine_final_norm",
    )(h, yg, tw, g)


def _t5_bucket(dist):
    n = jnp.maximum(dist, 0)
    max_exact = N_BUCKETS // 2
    nf = jnp.maximum(n, 1).astype(F32)
    large = max_exact + (jnp.log(nf / max_exact) / math.log(MAX_DISTANCE / max_exact)
                         * (N_BUCKETS - max_exact)).astype(jnp.int32)
    large = jnp.minimum(large, N_BUCKETS - 1)
    return jnp.where(n < max_exact, n, large)


def _bias_tables(rel_bias, nb):
    table = rel_bias.astype(F32)
    j = jnp.arange(BLK, dtype=jnp.int32)[:, None]
    s = jnp.arange(BLK, dtype=jnp.int32)[None, :]
    dist = jnp.concatenate([j + BLK - s, j - s], axis=1)
    valid = (dist >= 0) & (dist < WINDOW)
    main = jnp.where(valid[:, :, None], table[_t5_bucket(dist)], NEG)
    n_idx = jnp.arange(nb, dtype=jnp.int32)[:, None, None]
    m = jnp.arange(N_META, dtype=jnp.int32)[None, None, :]
    dist_m = N_META + n_idx * BLK + j[None] - m
    meta = table[_t5_bucket(dist_m)]
    meta = jnp.pad(meta, ((0, 0), (0, 0), (0, LANES - N_META), (0, 0)), constant_values=NEG)
    return main.transpose(2, 0, 1), meta.transpose(0, 3, 1, 2)


def _dup_heads(w):
    w = w.reshape(w.shape[0], N_KV_HEADS, 1, HEAD_DIM)
    return jnp.broadcast_to(w, (w.shape[0], N_KV_HEADS, 2, HEAD_DIM)).reshape(w.shape[0], KV_DUP)


def _pad_lanes(v, value=0.0):
    v = v.reshape(1, -1).astype(F32)
    return jnp.pad(v, ((0, 0), (0, LANES - v.shape[1])), constant_values=value)


def _row_tile(rows, target):
    tm = min(rows, target)
    assert rows % tm == 0
    return tm


def kernel(x, meta_tokens, rel_bias, norm_mix_g, w_in, attn_sinks, conv_w, conv_b, dt_bias, a_log,
           d_skip, ssm_norm_g, w_attn_br, w_ssm_br, w_out, norm_ffn_g, w_router, b_router,
           w_exp_in, b_exp_in, w_exp_out, b_exp_out, norm_final_g):
    bsz, seq, _ = x.shape
    assert seq % BLK == 0 and w_in.shape[0] == 1
    nb = seq // BLK
    rows = bsz * seq
    x2d = x.reshape(rows, D_MODEL)

    wi = w_in[0]
    o_k, o_v, o_z = Q_DIM, Q_DIM + KV_DIM, Q_DIM + 2 * KV_DIM
    o_x, o_dt = o_z + D_INNER, o_z + D_INNER + CONV_DIM
    o_g = o_dt + N_SSM_HEADS
    w_proj = jnp.concatenate([
        wi[:, :o_k] * (HEAD_DIM ** -0.5), _dup_heads(wi[:, o_k:o_v]), _dup_heads(wi[:, o_v:o_z]),
        wi[:, o_z:o_x], wi[:, o_x:o_dt], wi[:, o_g:]], axis=1).astype(BF16)
    w_dt = jnp.pad(wi[:, o_dt:o_g], ((0, 0), (0, LANES - N_SSM_HEADS))).astype(BF16)
    g_mix = norm_mix_g[0].reshape(1, D_MODEL)

    tm = _row_tile(rows, 512)
    q, k, v, z, xbc, gate, dtraw = _inproj(x2d, g_mix, w_proj, w_dt, tm)
    _, k_m, v_m, z_m, xbc_m, _, dtraw_m = _inproj(meta_tokens.astype(F32), g_mix, w_proj, w_dt,
                                                  N_META)

    pad_meta = ((BLK - N_META, 0), (0, 0))
    pad_meta_tail = ((0, BLK - N_META), (0, 0))
    bias_main, bias_meta = _bias_tables(rel_bias, nb)
    y_attn = _attention(q, k, v, jnp.pad(k_m, pad_meta_tail), jnp.pad(v_m, pad_meta_tail),
                        bias_main, bias_meta, attn_sinks[0].astype(F32), bsz, nb)

    ssd_params = (conv_w[0].astype(F32), conv_b[0].reshape(1, CONV_DIM).astype(F32),
                  _pad_lanes(dt_bias[0]), _pad_lanes(a_log[0]),
                  jnp.repeat(d_skip[0].astype(F32), SSM_HEAD_DIM).reshape(1, D_INNER),
                  ssm_norm_g[0].reshape(1, D_INNER).astype(F32))
    xbc_mp = jnp.pad(xbc_m, pad_meta)
    zero_state = jnp.zeros((N_SSM_PAIRS, D_STATE, LANES), F32)
    _, h_meta = _ssd(xbc_mp, jnp.pad(z_m, pad_meta), jnp.pad(dtraw_m, pad_meta),
                     jnp.zeros((8, CONV_DIM), F32), zero_state, *ssd_params, 1, 1,
                     BLK - N_META)
    y_ssm, _ = _ssd(xbc, z, dtraw, xbc_mp[BLK - 8:].astype(F32), h_meta[0], *ssd_params, bsz, nb,
                    0)

    w_r = jnp.pad(w_router[0], ((0, 0), (0, LANES - N_EXPERTS))).astype(BF16)
    b_r = _pad_lanes(b_router[0], NEG)
    h1, u2, top_idx, top_w = _merge(
        y_attn, y_ssm, gate, x2d, w_attn_br[0].astype(BF16), w_ssm_br[0].astype(BF16),
        w_out[0].astype(BF16), norm_ffn_g[0].reshape(1, D_MODEL), w_r, b_r, tm)

    tm_e = 256
    n_assign = rows * TOP_K
    n_tiles = -(-(n_assign + N_EXPERTS * (tm_e - 1)) // tm_e)
    flat_e = top_idx[:, :TOP_K].reshape(-1)
    order = jnp.argsort(flat_e)
    sorted_e = flat_e[order]
    counts = jnp.bincount(flat_e, length=N_EXPERTS)
    padded = (counts + tm_e - 1) // tm_e * tm_e
    start = jnp.cumsum(counts) - counts
    pend = jnp.cumsum(padded)
    pstart = pend - padded
    dest_sorted = (jnp.arange(n_assign, dtype=jnp.int32) + (pstart - start)[sorted_e]).astype(jnp.int32)
    row_tok = jnp.zeros((n_tiles * tm_e,), jnp.int32).at[dest_sorted].set(
        (order // TOP_K).astype(jnp.int32))
    dest = jnp.zeros((n_assign,), jnp.int32).at[order].set(dest_sorted)
    tile_e = jnp.minimum(
        jnp.searchsorted(pend, jnp.arange(n_tiles, dtype=jnp.int32) * tm_e, side='right'),
        N_EXPERTS - 1).astype(jnp.int32)
    n_valid = (pend[-1:] // tm_e).astype(jnp.int32)

    wi1 = w_exp_in[0]
    w1 = jnp.concatenate([wi1[..., 0::2], wi1[..., 1::2]], axis=-1).astype(BF16)
    bi1 = b_exp_in[0]
    b1 = jnp.concatenate([bi1[..., 0::2], bi1[..., 1::2]], axis=-1).reshape(N_EXPERTS, 1, 2 * D_FF)
    y_rows = _moe(tile_e, n_valid, u2[row_tok], w1, b1.astype(F32), w_exp_out[0].astype(BF16),
                  b_exp_out[0].reshape(N_EXPERTS, 1, D_MODEL).astype(F32), tm_e)

    yg = y_rows[dest].reshape(rows, TOP_K * D_MODEL)
    out = _final(h1, yg, top_w, norm_final_g.reshape(1, D_MODEL).astype(F32), tm)
    return out.reshape(bsz, seq, D_MODEL)
```

```python
import functools
import math

import jax
import jax.numpy as jnp
from jax import lax
from jax.experimental import pallas as pl
from jax.experimental.pallas import tpu as pltpu

F32 = jnp.float32
BF16 = jnp.bfloat16

D_MODEL = 1024
N_META = 16
N_HEADS = 16
N_KV_HEADS = 4
HEAD_DIM = 64
Q_PER_KV = N_HEADS // N_KV_HEADS
Q_DIM = N_HEADS * HEAD_DIM
KV_DIM = N_KV_HEADS * HEAD_DIM
WINDOW = 128
BLK = 128
N_BUCKETS = 32
MAX_DISTANCE = 128
D_INNER = 2 * D_MODEL
SSM_HEAD_DIM = 64
N_SSM_HEADS = D_INNER // SSM_HEAD_DIM
N_GROUPS = 4
D_STATE = 128
CONV_WIDTH = 4
BC_DIM = N_GROUPS * D_STATE
CONV_DIM = D_INNER + 2 * BC_DIM
N_EXPERTS = 32
TOP_K = 4
D_FF = D_MODEL
SWIGLU_ALPHA = 1.702
SWIGLU_LIMIT = 7.0
EPS = 1e-5
NEG = -1e30

LANES = 128
KV_DUP = 2 * KV_DIM
N_PAIRS = N_HEADS // 2
N_SSM_PAIRS = N_SSM_HEADS // 2
CTX = 3 * BLK
VMEM_LIMIT = 56 * 1024 * 1024

PROJ_SEGS = (("q", Q_DIM), ("k", KV_DUP), ("v", KV_DUP), ("z", D_INNER), ("xbc", CONV_DIM),
             ("gate", 2 * D_MODEL))
PROJ_N = sum(w for _, w in PROJ_SEGS)
PROJ_CHUNK = 512


def _dot(a, b):
    return jnp.dot(a, b, preferred_element_type=F32)


def _dot_nt(a, b):
    return lax.dot_general(a, b, (((1,), (1,)), ((), ())), preferred_element_type=F32)


def _sigmoid(x):
    return 1.0 / (1.0 + jnp.exp(-x))


def _const_spec(shape):
    nd = len(shape)
    return pl.BlockSpec(shape, lambda *_: (0,) * nd, pipeline_mode=pl.Buffered(1))


def _inproj_kernel(x_ref, g_ref, w_ref, wdt_ref, q_ref, k_ref, v_ref, z_ref, xbc_ref, gate_ref,
                   dt_ref):
    x = x_ref[...]
    ms = jnp.mean(x * x, axis=-1, keepdims=True)
    u = (x * lax.rsqrt(ms + EPS) * g_ref[...]).astype(BF16)
    outs = (q_ref, k_ref, v_ref, z_ref, xbc_ref, gate_ref)
    off = 0
    for ref, (_, width) in zip(outs, PROJ_SEGS):
        for c0 in range(0, width, PROJ_CHUNK):
            ref[:, c0:c0 + PROJ_CHUNK] = _dot(
                u, w_ref[:, off + c0:off + c0 + PROJ_CHUNK]).astype(BF16)
        off += width
    dt_ref[...] = _dot(u, wdt_ref[...])


def _inproj(x2d, g, w, wdt, tm):
    rows = x2d.shape[0]
    row = lambda i: (i, 0)
    out_shape = [jax.ShapeDtypeStruct((rows, wd), BF16) for _, wd in PROJ_SEGS]
    out_shape.append(jax.ShapeDtypeStruct((rows, LANES), F32))
    out_specs = [pl.BlockSpec((tm, wd), row) for _, wd in PROJ_SEGS]
    out_specs.append(pl.BlockSpec((tm, LANES), row))
    return pl.pallas_call(
        _inproj_kernel,
        grid=(rows // tm,),
        in_specs=[pl.BlockSpec((tm, D_MODEL), row), _const_spec((1, D_MODEL)),
                  _const_spec((D_MODEL, PROJ_N)), _const_spec((D_MODEL, LANES))],
        out_specs=out_specs,
        out_shape=out_shape,
        compiler_params=pltpu.CompilerParams(dimension_semantics=("parallel",),
                                             vmem_limit_bytes=VMEM_LIMIT),
        name="inproj",
    )(x2d, g, w, wdt)


def _attn_kernel(sink_ref, q_ref, kp_ref, kc_ref, km_ref, vp_ref, vc_ref, vm_ref, bmain_ref,
                 bmeta_ref, o_ref):
    n = pl.program_id(1)
    lane_ctx = lax.broadcasted_iota(jnp.int32, (1, CTX), 1)
    pen = jnp.where(jnp.logical_and(n == 0, lane_ctx < BLK), NEG, 0.0).astype(F32)
    lo = lax.broadcasted_iota(jnp.int32, (1, LANES), 1) < HEAD_DIM
    zero = jnp.zeros((), BF16)
    for g in range(N_KV_HEADS):
        cs = slice(g * LANES, (g + 1) * LANES)
        kk = jnp.concatenate([kp_ref[:, cs], kc_ref[:, cs], km_ref[:, cs]], axis=0)
        vv = jnp.concatenate([vp_ref[:, cs], vc_ref[:, cs], vm_ref[:, cs]], axis=0)
        k_lo = jnp.where(lo, kk, zero)
        k_hi = jnp.where(lo, zero, kk)
        v_bd = jnp.concatenate([jnp.where(lo, vv, zero), jnp.where(lo, zero, vv)], axis=0)
        for j in range(Q_PER_KV // 2):
            pi = g * (Q_PER_KV // 2) + j
            qp = q_ref[:, pi * LANES:(pi + 1) * LANES]
            ps, invs = [], []
            for t, kx in enumerate((k_lo, k_hi)):
                h = 2 * pi + t
                bias = jnp.concatenate([bmain_ref[h], bmeta_ref[0, h]], axis=1) + pen
                s = _dot_nt(qp, kx) + bias
                sink = sink_ref[h]
                m = jnp.maximum(jnp.max(s, axis=-1, keepdims=True), sink)
                p = jnp.exp(s - m)
                den = jnp.sum(p, axis=-1, keepdims=True) + jnp.exp(sink - m)
                ps.append(p.astype(BF16))
                invs.append(1.0 / den)
            o = _dot(jnp.concatenate(ps, axis=1), v_bd)
            o = o * jnp.where(lo, invs[0], invs[1])
            o_ref[:, pi * LANES:(pi + 1) * LANES] = o.astype(BF16)


def _attention(q, k, v, k_meta, v_meta, bias_main, bias_meta, sinks, bsz, nb):
    cur = lambda b, n, *_: (b * nb + n, 0)
    prev = lambda b, n, *_: (jnp.maximum(b * nb + n - 1, 0), 0)
    const2 = lambda b, n, *_: (0, 0)
    grid_spec = pltpu.PrefetchScalarGridSpec(
        num_scalar_prefetch=1,
        grid=(bsz, nb),
        in_specs=[
            pl.BlockSpec((BLK, Q_DIM), cur),
            pl.BlockSpec((BLK, KV_DUP), prev), pl.BlockSpec((BLK, KV_DUP), cur),
            pl.BlockSpec((BLK, KV_DUP), const2),
            pl.BlockSpec((BLK, KV_DUP), prev), pl.BlockSpec((BLK, KV_DUP), cur),
            pl.BlockSpec((BLK, KV_DUP), const2),
            pl.BlockSpec((N_HEADS, BLK, 2 * BLK), lambda b, n, *_: (0, 0, 0)),
            pl.BlockSpec((1, N_HEADS, BLK, LANES), lambda b, n, *_: (n, 0, 0, 0)),
        ],
        out_specs=pl.BlockSpec((BLK, Q_DIM), cur),
    )
    return pl.pallas_call(
        _attn_kernel,
        grid_spec=grid_spec,
        out_shape=jax.ShapeDtypeStruct((bsz * nb * BLK, Q_DIM), BF16),
        compiler_params=pltpu.CompilerParams(dimension_semantics=("parallel", "arbitrary"),
                                             vmem_limit_bytes=VMEM_LIMIT),
        name="swa_attention",
    )(sinks, q, k, k, k_meta, v, v, v_meta, bias_main, bias_meta)


CONV_COLS = 512


def _ssd_kernel(xbc_ref, z_ref, dtraw_ref, tail0_ref, h0_ref, convw_ref, convb_ref, dtb_ref,
                alog_ref, dskip_ref, ng_ref, y_ref, hfin_ref, win_ref, xc_ref, st_ref, *, n_pad):
    c = pl.program_id(1)

    @pl.when(c == 0)
    def _():
        win_ref[0:8, :] = tail0_ref[...]
        st_ref[...] = h0_ref[...]

    row = lax.broadcasted_iota(jnp.int32, (BLK, 1), 0)
    live = row >= n_pad
    win_ref[8:8 + BLK, :] = xbc_ref[...].astype(F32)
    for c0 in range(0, CONV_DIM, CONV_COLS):
        cs = slice(c0, c0 + CONV_COLS)
        acc = convb_ref[:, cs] + convw_ref[CONV_WIDTH - 1:CONV_WIDTH, cs] * win_ref[8:8 + BLK, cs]
        for w in range(CONV_WIDTH - 1):
            s0 = 8 - (CONV_WIDTH - 1) + w
            acc = acc + convw_ref[w:w + 1, cs] * win_ref[s0:s0 + BLK, cs]
        act = acc * _sigmoid(acc)
        if n_pad:
            act = jnp.where(live, act, 0.0)
        xc_ref[:, cs] = act
    win_ref[0:8, :] = win_ref[BLK:BLK + 8, :]

    x = dtraw_ref[...] + dtb_ref[...]
    dt = jnp.maximum(x, 0.0) + jnp.log1p(jnp.exp(-jnp.abs(x)))
    if n_pad:
        dt = jnp.where(live, dt, 0.0)
    a = dt * (-jnp.exp(alog_ref[...]))
    ri = lax.broadcasted_iota(jnp.int32, (BLK, BLK), 0)
    ci = lax.broadcasted_iota(jnp.int32, (BLK, BLK), 1)
    causal = ri >= ci
    tril = causal.astype(F32)
    triu = (ri <= ci).astype(F32)
    hi = lax.Precision.HIGHEST
    a_cs = jnp.dot(tril, a, precision=hi, preferred_element_type=F32)
    a_t = a.T[:N_SSM_HEADS]
    dt_t = dt.T[:N_SSM_HEADS]
    a_cs_t = jnp.dot(a_t, triu, precision=hi, preferred_element_type=F32)
    w_t = dt_t * jnp.exp(a_cs_t[:, BLK - 1:BLK] - a_cs_t)
    e_cs = jnp.exp(a_cs)
    lo = lax.broadcasted_iota(jnp.int32, (1, LANES), 1) < SSM_HEAD_DIM

    pairs_per_group = N_SSM_PAIRS // N_GROUPS
    for g in range(N_GROUPS):
        b_g = xc_ref[:, D_INNER + g * D_STATE:D_INNER + (g + 1) * D_STATE]
        c_g = xc_ref[:, D_INNER + BC_DIM + g * D_STATE:D_INNER + BC_DIM + (g + 1) * D_STATE]
        cb = _dot_nt(c_g.astype(BF16), b_g.astype(BF16))
        b_gt = b_g.T
        ys = []
        for j in range(pairs_per_group):
            pi = g * pairs_per_group + j
            ps = slice(pi * LANES, (pi + 1) * LANES)
            xs = xc_ref[:, ps]
            xs_b = xs.astype(BF16)
            st = st_ref[pi]
            rhs = jnp.concatenate([xs_b, st.astype(BF16)], axis=0)
            y_h, st_h = [], []
            for t in range(2):
                h = 2 * pi + t
                diff = a_cs[:, h:h + 1] - a_cs_t[h:h + 1, :]
                dec = jnp.exp(jnp.where(causal, diff, -jnp.inf))
                m_h = cb * dec * dt_t[h:h + 1, :]
                ce = c_g * e_cs[:, h:h + 1]
                lhs = jnp.concatenate([m_h, ce], axis=1).astype(BF16)
                y_h.append(_dot(lhs, rhs))
                s_new = _dot((b_gt * w_t[h:h + 1, :]).astype(BF16), xs_b)
                st_h.append(st * e_cs[BLK - 1:BLK, h:h + 1] + s_new)
            st_ref[pi] = jnp.where(lo, st_h[0], st_h[1])
            ys.append(jnp.where(lo, y_h[0], y_h[1]) + dskip_ref[:, ps] * xs)
        gs = slice(g * (D_INNER // N_GROUPS), (g + 1) * (D_INNER // N_GROUPS))
        zg = z_ref[:, gs].astype(F32)
        yg = jnp.concatenate(ys, axis=1) * (zg * _sigmoid(zg))
        ms = jnp.mean(yg * yg, axis=-1, keepdims=True)
        y_ref[:, gs] = (yg * lax.rsqrt(ms + EPS) * ng_ref[:, gs]).astype(BF16)

    @pl.when(c == pl.num_programs(1) - 1)
    def _():
        hfin_ref[0] = st_ref[...]


def _ssd(xbc, z, dtraw, tail0, h0, convw, convb, dtb, alog, dskip, ng, bsz, nc, n_pad):
    cur = lambda b, c: (b * nc + c, 0)
    return pl.pallas_call(
        functools.partial(_ssd_kernel, n_pad=n_pad),
        grid=(bsz, nc),
        in_specs=[
            pl.BlockSpec((BLK, CONV_DIM), cur), pl.BlockSpec((BLK, D_INNER), cur),
            pl.BlockSpec((BLK, LANES), cur),
            _const_spec((8, CONV_DIM)), _const_spec((N_SSM_PAIRS, D_STATE, LANES)),
            _const_spec((CONV_WIDTH, CONV_DIM)), _const_spec((1, CONV_DIM)),
            _const_spec((1, LANES)), _const_spec((1, LANES)), _const_spec((1, D_INNER)),
            _const_spec((1, D_INNER)),
        ],
        out_specs=[pl.BlockSpec((BLK, D_INNER), cur),
                   pl.BlockSpec((1, N_SSM_PAIRS, D_STATE, LANES), lambda b, c: (b, 0, 0, 0))],
        out_shape=[jax.ShapeDtypeStruct((bsz * nc * BLK, D_INNER), BF16),
                   jax.ShapeDtypeStruct((bsz, N_SSM_PAIRS, D_STATE, LANES), F32)],
        scratch_shapes=[pltpu.VMEM((BLK + 8, CONV_DIM), F32), pltpu.VMEM((BLK, CONV_DIM), F32),
                        pltpu.VMEM((N_SSM_PAIRS, D_STATE, LANES), F32)],
        compiler_params=pltpu.CompilerParams(dimension_semantics=("parallel", "arbitrary"),
                                             vmem_limit_bytes=VMEM_LIMIT),
        name="ssd_mixer",
    )(xbc, z, dtraw, tail0, h0, convw, convb, dtb, alog, dskip, ng)


def _merge_kernel(ya_ref, ys_ref, gl_ref, x_ref, wa_ref, ws_ref, wo_ref, g_ref, wr_ref, br_ref,
                  h_ref, u_ref, idx_ref, tw_ref):
    g0 = _sigmoid(gl_ref[:, :D_MODEL].astype(F32))
    g1 = _sigmoid(gl_ref[:, D_MODEL:].astype(F32))
    merged = g0 * _dot(ya_ref[...], wa_ref[...]) + g1 * _dot(ys_ref[...], ws_ref[...])
    h = x_ref[...] + _dot(merged.astype(BF16), wo_ref[...])
    h_ref[...] = h
    ms = jnp.mean(h * h, axis=-1, keepdims=True)
    u = (h * lax.rsqrt(ms + EPS) * g_ref[...]).astype(BF16)
    u_ref[...] = u
    logits = _dot(u, wr_ref[...]) + br_ref[...]
    lane = lax.broadcasted_iota(jnp.int32, logits.shape, 1).astype(F32)
    idx_out = jnp.zeros(logits.shape, F32)
    val_out = jnp.zeros(logits.shape, F32)
    top = None
    den = None
    for k in range(TOP_K):
        m = jnp.max(logits, axis=-1, keepdims=True)
        sel = jnp.min(jnp.where(logits == m, lane, float(LANES)), axis=-1, keepdims=True)
        if k == 0:
            top = m
        e = jnp.exp(m - top)
        den = e if k == 0 else den + e
        idx_out = jnp.where(lane == k, sel, idx_out)
        val_out = jnp.where(lane == k, e, val_out)
        logits = jnp.where(lane == sel, -jnp.inf, logits)
    idx_ref[...] = idx_out.astype(jnp.int32)
    tw_ref[...] = val_out / den


def _merge(ya, ys, gl, x2d, wa, ws, wo, g, wr, br, tm):
    rows = x2d.shape[0]
    row = lambda i: (i, 0)
    return pl.pallas_call(
        _merge_kernel,
        grid=(rows // tm,),
        in_specs=[pl.BlockSpec((tm, Q_DIM), row), pl.BlockSpec((tm, D_INNER), row),
                  pl.BlockSpec((tm, 2 * D_MODEL), row), pl.BlockSpec((tm, D_MODEL), row),
                  _const_spec((Q_DIM, D_MODEL)), _const_spec((D_INNER, D_MODEL)),
                  _const_spec((D_MODEL, D_MODEL)), _const_spec((1, D_MODEL)),
                  _const_spec((D_MODEL, LANES)), _const_spec((1, LANES))],
        out_specs=[pl.BlockSpec((tm, D_MODEL), row), pl.BlockSpec((tm, D_MODEL), row),
                   pl.BlockSpec((tm, LANES), row), pl.BlockSpec((tm, LANES), row)],
        out_shape=[jax.ShapeDtypeStruct((rows, D_MODEL), F32),
                   jax.ShapeDtypeStruct((rows, D_MODEL), BF16),
                   jax.ShapeDtypeStruct((rows, LANES), jnp.int32),
                   jax.ShapeDtypeStruct((rows, LANES), F32)],
        compiler_params=pltpu.CompilerParams(dimension_semantics=("parallel",),
                                             vmem_limit_bytes=VMEM_LIMIT),
        name="merge_router",
    )(ya, ys, gl, x2d, wa, ws, wo, g, wr, br)


def _route_kernel(idx_ref, dest_ref, info_ref, cnt_ref, base_ref, *, tm_e):
    p = pl.program_id(0)
    i = pl.program_id(1)
    tr = idx_ref.shape[0]
    lane = lax.broadcasted_iota(jnp.int32, (tr, LANES), 1)
    idx = idx_ref[...]
    onehot = [jnp.where(lane == idx[:, k:k + 1], 1.0, 0.0) for k in range(TOP_K)]
    oh_all = onehot[0] + onehot[1] + onehot[2] + onehot[3]
    tile_cnt = jnp.sum(oh_all, axis=0, keepdims=True)

    @pl.when(jnp.logical_and(p == 0, i == 0))
    def _():
        cnt_ref[...] = jnp.zeros_like(cnt_ref)

    @pl.when(p == 0)
    def _():
        cnt_ref[...] += tile_cnt

    @pl.when(jnp.logical_and(p == 1, i == 0))
    def _():
        counts = jnp.broadcast_to(cnt_ref[...], (8, LANES))
        padded = jnp.floor((counts + (tm_e - 1)) * (1.0 / tm_e)) * tm_e
        lane8 = lax.broadcasted_iota(jnp.int32, (8, LANES), 1)
        incl = padded
        shift = 1
        while shift < LANES:
            incl = incl + jnp.where(lane8 >= shift, pltpu.roll(incl, shift, 1), 0.0)
            shift *= 2
        pstart = incl - padded
        base_ref[...] = pstart[0:1]
        row8 = lax.broadcasted_iota(jnp.int32, (8, LANES), 0)
        info = jnp.where(row8 == 0, counts, jnp.where(row8 == 1, pstart, incl))
        info_ref[...] = info.astype(jnp.int32)

    @pl.when(p == 1)
    def _():
        r = lax.broadcasted_iota(jnp.int32, (tr, tr), 0)
        c = lax.broadcasted_iota(jnp.int32, (tr, tr), 1)
        before = jnp.where(r > c, 1.0, 0.0).astype(BF16)
        earlier = _dot(before, oh_all.astype(BF16))
        rank = base_ref[...] + earlier
        dest = jnp.zeros((tr, LANES), F32)
        for k in range(TOP_K):
            d_k = jnp.sum(onehot[k] * rank, axis=-1, keepdims=True)
            dest = jnp.where(lane == k, d_k, dest)
        dest_ref[...] = dest.T[0:8].astype(jnp.int32)
        base_ref[...] += tile_cnt


def _route(top_idx, tm_e, tr):
    rows = top_idx.shape[0]
    return pl.pallas_call(
        functools.partial(_route_kernel, tm_e=tm_e),
        grid=(2, rows // tr),
        in_specs=[pl.BlockSpec((tr, LANES), lambda p, i: (i, 0))],
        out_specs=[pl.BlockSpec((8, tr), lambda p, i: (0, i * p)),
                   pl.BlockSpec((8, LANES), lambda p, i: (0, 0))],
        out_shape=[jax.ShapeDtypeStruct((8, rows), jnp.int32),
                   jax.ShapeDtypeStruct((8, LANES), jnp.int32)],
        scratch_shapes=[pltpu.VMEM((1, LANES), F32), pltpu.VMEM((1, LANES), F32)],
        compiler_params=pltpu.CompilerParams(dimension_semantics=("arbitrary", "arbitrary"),
                                             vmem_limit_bytes=VMEM_LIMIT),
        name="route",
    )(top_idx)


FF_CHUNK = 2 * LANES


def _moe_kernel(te_ref, nv_ref, x_ref, w1_ref, b1_ref, w2_ref, b2_ref, y_ref, w1s_ref, w2s_ref):
    i = pl.program_id(0)
    valid = i < nv_ref[0]
    fresh = jnp.logical_or(i == 0, te_ref[i] != te_ref[jnp.maximum(i - 1, 0)])

    @pl.when(jnp.logical_and(valid, fresh))
    def _():
        r = lax.broadcasted_iota(jnp.int32, (FF_CHUNK, FF_CHUNK), 0)
        c = lax.broadcasted_iota(jnp.int32, (FF_CHUNK, FF_CHUNK), 1)
        src = jnp.where(c < LANES, 2 * c, 2 * (c - LANES) + 1)
        perm = jnp.where(r == src, 1.0, 0.0).astype(BF16)
        for c0 in range(0, 2 * D_FF, FF_CHUNK):
            cs = slice(c0, c0 + FF_CHUNK)
            w1s_ref[:, cs] = _dot(w1_ref[0, :, cs].astype(BF16), perm).astype(BF16)
        w2s_ref[...] = w2_ref[0].astype(BF16)

    @pl.when(valid)
    def _():
        x = x_ref[...]
        acts = []
        for c0 in range(0, 2 * D_FF, FF_CHUNK):
            cs = slice(c0, c0 + FF_CHUNK)
            hh = _dot(x, w1s_ref[:, cs]) + b1_ref[0, :, cs]
            glu = jnp.minimum(hh[:, :LANES], SWIGLU_LIMIT)
            lin = jnp.clip(hh[:, LANES:], -SWIGLU_LIMIT, SWIGLU_LIMIT)
            acts.append((glu * _sigmoid(SWIGLU_ALPHA * glu) * (lin + 1.0)).astype(BF16))
        act = jnp.concatenate(acts, axis=1)
        y_ref[...] = (_dot(act, w2s_ref[...]) + b2_ref[0]).astype(BF16)

    @pl.when(jnp.logical_not(valid))
    def _():
        y_ref[...] = jnp.zeros_like(y_ref)


def _moe(tile_e, n_valid, xrows, w1, b1, w2, b2, tm):
    n_tiles = xrows.shape[0] // tm
    wsel = lambda i, te, nv: (te[i], 0, 0)
    grid_spec = pltpu.PrefetchScalarGridSpec(
        num_scalar_prefetch=2,
        grid=(n_tiles,),
        in_specs=[pl.BlockSpec((tm, D_MODEL), lambda i, te, nv: (i, 0)),
                  pl.BlockSpec((1, D_MODEL, 2 * D_FF), wsel), pl.BlockSpec((1, 1, 2 * D_FF), wsel),
                  pl.BlockSpec((1, D_FF, D_MODEL), wsel), pl.BlockSpec((1, 1, D_MODEL), wsel)],
        out_specs=pl.BlockSpec((tm, D_MODEL), lambda i, te, nv: (i, 0)),
        scratch_shapes=[pltpu.VMEM((D_MODEL, 2 * D_FF), BF16), pltpu.VMEM((D_FF, D_MODEL), BF16)],
    )
    return pl.pallas_call(
        _moe_kernel,
        grid_spec=grid_spec,
        out_shape=jax.ShapeDtypeStruct(xrows.shape, BF16),
        compiler_params=pltpu.CompilerParams(dimension_semantics=("arbitrary",),
                                             vmem_limit_bytes=VMEM_LIMIT),
        name="moe_experts",
    )(tile_e, n_valid, xrows, w1, b1, w2, b2)


def _final_kernel(h_ref, y0_ref, y1_ref, y2_ref, y3_ref, tw_ref, g_ref, o_ref):
    h = h_ref[...]
    tw = tw_ref[...]
    for k, y_ref in enumerate((y0_ref, y1_ref, y2_ref, y3_ref)):
        h = h + tw[:, k:k + 1] * y_ref[...].astype(F32)
    ms = jnp.mean(h * h, axis=-1, keepdims=True)
    o_ref[...] = h * lax.rsqrt(ms + EPS) * g_ref[...]


def _final(h, yg, tw, g, tm):
    rows = h.shape[0]
    row = lambda i: (i, 0)
    return pl.pallas_call(
        _final_kernel,
        grid=(rows // tm,),
        in_specs=[pl.BlockSpec((tm, D_MODEL), row)] * (1 + TOP_K)
        + [pl.BlockSpec((tm, LANES), row), _const_spec((1, D_MODEL))],
        out_specs=pl.BlockSpec((tm, D_MODEL), row),
        out_shape=jax.ShapeDtypeStruct((rows, D_MODEL), F32),
        compiler_params=pltpu.CompilerParams(dimension_semantics=("parallel",),
                                             vmem_limit_bytes=VMEM_LIMIT),
        name="combine_final_norm",
    )(h, *yg, tw, g)


def _t5_bucket(dist):
    n = jnp.maximum(dist, 0)
    max_exact = N_BUCKETS // 2
    nf = jnp.maximum(n, 1).astype(F32)
    large = max_exact + (jnp.log(nf / max_exact) / math.log(MAX_DISTANCE / max_exact)
                         * (N_BUCKETS - max_exact)).astype(jnp.int32)
    large = jnp.minimum(large, N_BUCKETS - 1)
    return jnp.where(n < max_exact, n, large)


def _bias_tables(rel_bias, nb):
    table = rel_bias.astype(F32)
    j = jnp.arange(BLK, dtype=jnp.int32)[:, None]
    s = jnp.arange(BLK, dtype=jnp.int32)[None, :]
    dist = jnp.concatenate([j + BLK - s, j - s], axis=1)
    valid = (dist >= 0) & (dist < WINDOW)
    main = jnp.where(valid[:, :, None], table[_t5_bucket(dist)], NEG)
    n_idx = jnp.arange(nb, dtype=jnp.int32)[:, None, None]
    m = jnp.arange(N_META, dtype=jnp.int32)[None, None, :]
    dist_m = N_META + n_idx * BLK + j[None] - m
    meta = table[_t5_bucket(dist_m)]
    meta = jnp.pad(meta, ((0, 0), (0, 0), (0, LANES - N_META), (0, 0)), constant_values=NEG)
    return main.transpose(2, 0, 1), meta.transpose(0, 3, 1, 2)


def _dup_heads(w):
    w = w.reshape(w.shape[0], N_KV_HEADS, 1, HEAD_DIM)
    return jnp.broadcast_to(w, (w.shape[0], N_KV_HEADS, 2, HEAD_DIM)).reshape(w.shape[0], KV_DUP)


def _pad_lanes(v, value=0.0):
    v = v.reshape(1, -1).astype(F32)
    return jnp.pad(v, ((0, 0), (0, LANES - v.shape[1])), constant_values=value)


def _row_tile(rows, target):
    tm = min(rows, target)
    assert rows % tm == 0
    return tm


def kernel(x, meta_tokens, rel_bias, norm_mix_g, w_in, attn_sinks, conv_w, conv_b, dt_bias, a_log,
           d_skip, ssm_norm_g, w_attn_br, w_ssm_br, w_out, norm_ffn_g, w_router, b_router,
           w_exp_in, b_exp_in, w_exp_out, b_exp_out, norm_final_g):
    bsz, seq, _ = x.shape
    assert seq % BLK == 0 and w_in.shape[0] == 1
    nb = seq // BLK
    rows = bsz * seq
    x2d = x.reshape(rows, D_MODEL)

    wi = w_in[0]
    o_k, o_v, o_z = Q_DIM, Q_DIM + KV_DIM, Q_DIM + 2 * KV_DIM
    o_x, o_dt = o_z + D_INNER, o_z + D_INNER + CONV_DIM
    o_g = o_dt + N_SSM_HEADS
    w_proj = jnp.concatenate([
        wi[:, :o_k] * (HEAD_DIM ** -0.5), _dup_heads(wi[:, o_k:o_v]), _dup_heads(wi[:, o_v:o_z]),
        wi[:, o_z:o_x], wi[:, o_x:o_dt], wi[:, o_g:]], axis=1).astype(BF16)
    w_dt = jnp.pad(wi[:, o_dt:o_g], ((0, 0), (0, LANES - N_SSM_HEADS))).astype(BF16)
    g_mix = norm_mix_g[0].reshape(1, D_MODEL)

    tm = _row_tile(rows, 512)
    q, k, v, z, xbc, gate, dtraw = _inproj(x2d, g_mix, w_proj, w_dt, tm)
    _, k_m, v_m, z_m, xbc_m, _, dtraw_m = _inproj(meta_tokens.astype(F32), g_mix, w_proj, w_dt,
                                                  N_META)

    pad_meta = ((BLK - N_META, 0), (0, 0))
    pad_meta_tail = ((0, BLK - N_META), (0, 0))
    bias_main, bias_meta = _bias_tables(rel_bias, nb)
    y_attn = _attention(q, k, v, jnp.pad(k_m, pad_meta_tail), jnp.pad(v_m, pad_meta_tail),
                        bias_main, bias_meta, attn_sinks[0].astype(F32), bsz, nb)

    ssd_params = (conv_w[0].astype(F32), conv_b[0].reshape(1, CONV_DIM).astype(F32),
                  _pad_lanes(dt_bias[0]), _pad_lanes(a_log[0]),
                  jnp.repeat(d_skip[0].astype(F32), SSM_HEAD_DIM).reshape(1, D_INNER),
                  ssm_norm_g[0].reshape(1, D_INNER).astype(F32))
    xbc_mp = jnp.pad(xbc_m, pad_meta)
    zero_state = jnp.zeros((N_SSM_PAIRS, D_STATE, LANES), F32)
    _, h_meta = _ssd(xbc_mp, jnp.pad(z_m, pad_meta), jnp.pad(dtraw_m, pad_meta),
                     jnp.zeros((8, CONV_DIM), F32), zero_state, *ssd_params, 1, 1,
                     BLK - N_META)
    y_ssm, _ = _ssd(xbc, z, dtraw, xbc_mp[BLK - 8:].astype(F32), h_meta[0], *ssd_params, bsz, nb,
                    0)

    w_r = jnp.pad(w_router[0], ((0, 0), (0, LANES - N_EXPERTS))).astype(BF16)
    b_r = _pad_lanes(b_router[0], NEG)
    h1, u2, top_idx, top_w = _merge(
        y_attn, y_ssm, gate, x2d, w_attn_br[0].astype(BF16), w_ssm_br[0].astype(BF16),
        w_out[0].astype(BF16), norm_ffn_g[0].reshape(1, D_MODEL), w_r, b_r, tm)

    tm_e = 256
    n_tiles = -(-(rows * TOP_K + N_EXPERTS * (tm_e - 1)) // tm_e)
    dest_t, info = _route(top_idx, tm_e, tm)
    pend = info[2, :N_EXPERTS]
    row_tok = jnp.zeros((n_tiles * tm_e,), jnp.int32).at[dest_t[:TOP_K].reshape(-1)].set(
        jnp.tile(jnp.arange(rows, dtype=jnp.int32), TOP_K))
    tile_start = jnp.arange(n_tiles, dtype=jnp.int32) * tm_e
    tile_e = jnp.minimum(jnp.sum(tile_start[:, None] >= pend[None, :], axis=1),
                         N_EXPERTS - 1).astype(jnp.int32)
    n_valid = pend[-1:] // tm_e

    b1 = b_exp_in[0].astype(F32).reshape(N_EXPERTS, 2 * D_FF // FF_CHUNK, LANES, 2)
    b1 = b1.transpose(0, 1, 3, 2).reshape(N_EXPERTS, 1, 2 * D_FF)
    y_rows = _moe(tile_e, n_valid, u2[row_tok], w_exp_in[0], b1, w_exp_out[0],
                  b_exp_out[0].reshape(N_EXPERTS, 1, D_MODEL).astype(F32), tm_e)

    yg = [y_rows[dest_t[k]] for k in range(TOP_K)]
    out = _final(h1, yg, top_w, norm_final_g.reshape(1, D_MODEL).astype(F32), tm)
    return out.reshape(bsz, seq, D_MODEL)
```

```python
import functools
import math

import jax
import jax.numpy as jnp
from jax import lax
from jax.experimental import pallas as pl
from jax.experimental.pallas import tpu as pltpu

F32 = jnp.float32
BF16 = jnp.bfloat16

D_MODEL = 1024
N_META = 16
N_HEADS = 16
N_KV_HEADS = 4
HEAD_DIM = 64
Q_PER_KV = N_HEADS // N_KV_HEADS
Q_DIM = N_HEADS * HEAD_DIM
KV_DIM = N_KV_HEADS * HEAD_DIM
WINDOW = 128
BLK = 128
N_BUCKETS = 32
MAX_DISTANCE = 128
D_INNER = 2 * D_MODEL
SSM_HEAD_DIM = 64
N_SSM_HEADS = D_INNER // SSM_HEAD_DIM
N_GROUPS = 4
D_STATE = 128
CONV_WIDTH = 4
BC_DIM = N_GROUPS * D_STATE
CONV_DIM = D_INNER + 2 * BC_DIM
N_EXPERTS = 32
TOP_K = 4
D_FF = D_MODEL
SWIGLU_ALPHA = 1.702
SWIGLU_LIMIT = 7.0
EPS = 1e-5
NEG = -1e30

LANES = 128
KV_DUP = 2 * KV_DIM
N_PAIRS = N_HEADS // 2
N_SSM_PAIRS = N_SSM_HEADS // 2
CTX = 3 * BLK
VMEM_LIMIT = 56 * 1024 * 1024

PROJ_SEGS = (("q", Q_DIM), ("k", KV_DUP), ("v", KV_DUP), ("z", D_INNER), ("xbc", CONV_DIM),
             ("gate", 2 * D_MODEL))
PROJ_N = sum(w for _, w in PROJ_SEGS)
PROJ_CHUNK = 512


def _dot(a, b):
    return jnp.dot(a, b, preferred_element_type=F32)


def _dot_nt(a, b):
    return lax.dot_general(a, b, (((1,), (1,)), ((), ())), preferred_element_type=F32)


def _sigmoid(x):
    return 1.0 / (1.0 + jnp.exp(-x))


def _const_spec(shape):
    nd = len(shape)
    return pl.BlockSpec(shape, lambda *_: (0,) * nd, pipeline_mode=pl.Buffered(1))


def _inproj_kernel(x_ref, g_ref, w_ref, wdt_ref, q_ref, k_ref, v_ref, z_ref, xbc_ref, gate_ref,
                   dt_ref):
    x = x_ref[...]
    ms = jnp.mean(x * x, axis=-1, keepdims=True)
    u = (x * lax.rsqrt(ms + EPS) * g_ref[...]).astype(BF16)
    outs = (q_ref, k_ref, v_ref, z_ref, xbc_ref, gate_ref)
    off = 0
    for ref, (_, width) in zip(outs, PROJ_SEGS):
        for c0 in range(0, width, PROJ_CHUNK):
            ref[:, c0:c0 + PROJ_CHUNK] = _dot(
                u, w_ref[:, off + c0:off + c0 + PROJ_CHUNK]).astype(BF16)
        off += width
    dt_ref[...] = _dot(u, wdt_ref[...])


def _inproj(x2d, g, w, wdt, tm):
    rows = x2d.shape[0]
    row = lambda i: (i, 0)
    out_shape = [jax.ShapeDtypeStruct((rows, wd), BF16) for _, wd in PROJ_SEGS]
    out_shape.append(jax.ShapeDtypeStruct((rows, LANES), F32))
    out_specs = [pl.BlockSpec((tm, wd), row) for _, wd in PROJ_SEGS]
    out_specs.append(pl.BlockSpec((tm, LANES), row))
    return pl.pallas_call(
        _inproj_kernel,
        grid=(rows // tm,),
        in_specs=[pl.BlockSpec((tm, D_MODEL), row), _const_spec((1, D_MODEL)),
                  _const_spec((D_MODEL, PROJ_N)), _const_spec((D_MODEL, LANES))],
        out_specs=out_specs,
        out_shape=out_shape,
        compiler_params=pltpu.CompilerParams(dimension_semantics=("parallel",),
                                             vmem_limit_bytes=VMEM_LIMIT),
        name="inproj",
    )(x2d, g, w, wdt)


def _attn_kernel(sink_ref, q_ref, kp_ref, kc_ref, km_ref, vp_ref, vc_ref, vm_ref, bmain_ref,
                 bmeta_ref, o_ref):
    n = pl.program_id(1)
    lane_ctx = lax.broadcasted_iota(jnp.int32, (1, CTX), 1)
    pen = jnp.where(jnp.logical_and(n == 0, lane_ctx < BLK), NEG, 0.0).astype(F32)
    lo = lax.broadcasted_iota(jnp.int32, (1, LANES), 1) < HEAD_DIM
    zero = jnp.zeros((), BF16)
    for g in range(N_KV_HEADS):
        cs = slice(g * LANES, (g + 1) * LANES)
        kk = jnp.concatenate([kp_ref[:, cs], kc_ref[:, cs], km_ref[:, cs]], axis=0)
        vv = jnp.concatenate([vp_ref[:, cs], vc_ref[:, cs], vm_ref[:, cs]], axis=0)
        k_lo = jnp.where(lo, kk, zero)
        k_hi = jnp.where(lo, zero, kk)
        v_bd = jnp.concatenate([jnp.where(lo, vv, zero), jnp.where(lo, zero, vv)], axis=0)
        for j in range(Q_PER_KV // 2):
            pi = g * (Q_PER_KV // 2) + j
            qp = q_ref[:, pi * LANES:(pi + 1) * LANES]
            ps, invs = [], []
            for t, kx in enumerate((k_lo, k_hi)):
                h = 2 * pi + t
                bias = jnp.concatenate([bmain_ref[h], bmeta_ref[0, h]], axis=1) + pen
                s = _dot_nt(qp, kx) + bias
                sink = sink_ref[h]
                m = jnp.maximum(jnp.max(s, axis=-1, keepdims=True), sink)
                p = jnp.exp(s - m)
                den = jnp.sum(p, axis=-1, keepdims=True) + jnp.exp(sink - m)
                ps.append(p.astype(BF16))
                invs.append(1.0 / den)
            o = _dot(jnp.concatenate(ps, axis=1), v_bd)
            o = o * jnp.where(lo, invs[0], invs[1])
            o_ref[:, pi * LANES:(pi + 1) * LANES] = o.astype(BF16)


def _attention(q, k, v, k_meta, v_meta, bias_main, bias_meta, sinks, bsz, nb):
    cur = lambda b, n, *_: (b * nb + n, 0)
    prev = lambda b, n, *_: (jnp.maximum(b * nb + n - 1, 0), 0)
    const2 = lambda b, n, *_: (0, 0)
    grid_spec = pltpu.PrefetchScalarGridSpec(
        num_scalar_prefetch=1,
        grid=(bsz, nb),
        in_specs=[
            pl.BlockSpec((BLK, Q_DIM), cur),
            pl.BlockSpec((BLK, KV_DUP), prev), pl.BlockSpec((BLK, KV_DUP), cur),
            pl.BlockSpec((BLK, KV_DUP), const2),
            pl.BlockSpec((BLK, KV_DUP), prev), pl.BlockSpec((BLK, KV_DUP), cur),
            pl.BlockSpec((BLK, KV_DUP), const2),
            pl.BlockSpec((N_HEADS, BLK, 2 * BLK), lambda b, n, *_: (0, 0, 0)),
            pl.BlockSpec((1, N_HEADS, BLK, LANES), lambda b, n, *_: (n, 0, 0, 0)),
        ],
        out_specs=pl.BlockSpec((BLK, Q_DIM), cur),
    )
    return pl.pallas_call(
        _attn_kernel,
        grid_spec=grid_spec,
        out_shape=jax.ShapeDtypeStruct((bsz * nb * BLK, Q_DIM), BF16),
        compiler_params=pltpu.CompilerParams(dimension_semantics=("parallel", "arbitrary"),
                                             vmem_limit_bytes=VMEM_LIMIT),
        name="swa_attention",
    )(sinks, q, k, k, k_meta, v, v, v_meta, bias_main, bias_meta)


CONV_COLS = 256
TAIL = 16


def _ssd_kernel(xbc_ref, z_ref, dtraw_ref, tail0_ref, h0_ref, convw_ref, convb_ref, dtb_ref,
                alog_ref, dskip_ref, ng_ref, y_ref, hfin_ref, win_ref, xc_ref, st_ref, *, n_pad):
    c = pl.program_id(1)

    @pl.when(c == 0)
    def _():
        win_ref[0:TAIL, :] = tail0_ref[...]
        st_ref[...] = h0_ref[...]

    win_ref[TAIL:TAIL + BLK, :] = xbc_ref[...]
    sr = lax.broadcasted_iota(jnp.int32, (BLK, TAIL + BLK), 0)
    sc = lax.broadcasted_iota(jnp.int32, (BLK, TAIL + BLK), 1)
    shifts = [jnp.where(sc == sr + (TAIL - (CONV_WIDTH - 1) + w), 1.0, 0.0).astype(BF16)
              for w in range(CONV_WIDTH - 1)]
    live = lax.broadcasted_iota(jnp.int32, (BLK, 1), 0) >= n_pad
    for c0 in range(0, CONV_DIM, CONV_COLS):
        cs = slice(c0, c0 + CONV_COLS)
        wv = win_ref[:, cs]
        acc = convb_ref[:, cs] + convw_ref[CONV_WIDTH - 1:CONV_WIDTH, cs] * wv[TAIL:].astype(F32)
        for w in range(CONV_WIDTH - 1):
            acc = acc + convw_ref[w:w + 1, cs] * _dot(shifts[w], wv)
        act = acc * _sigmoid(acc)
        if n_pad:
            act = jnp.where(live, act, 0.0)
        xc_ref[:, cs] = act
    win_ref[0:TAIL, :] = win_ref[BLK:BLK + TAIL, :]

    lane_t = lax.broadcasted_iota(jnp.int32, (N_SSM_HEADS, BLK), 1)
    x_t = dtraw_ref[...].T[:N_SSM_HEADS] + dtb_ref[...]
    dt_t = jnp.maximum(x_t, 0.0) + jnp.log1p(jnp.exp(-jnp.abs(x_t)))
    if n_pad:
        dt_t = jnp.where(lane_t >= n_pad, dt_t, 0.0)
    a_cs_t = dt_t * (-jnp.exp(alog_ref[...]))
    shift = 1
    while shift < BLK:
        a_cs_t = a_cs_t + jnp.where(lane_t >= shift, pltpu.roll(a_cs_t, shift, 1), 0.0)
        shift *= 2
    w_t = dt_t * jnp.exp(a_cs_t[:, BLK - 1:BLK] - a_cs_t)
    fill = jnp.zeros((BLK - N_SSM_HEADS, BLK), F32)
    a_cs = jnp.concatenate([a_cs_t, fill], axis=0).T
    e_cs = jnp.exp(a_cs)
    ri = lax.broadcasted_iota(jnp.int32, (BLK, BLK), 0)
    ci = lax.broadcasted_iota(jnp.int32, (BLK, BLK), 1)
    causal = ri >= ci
    lo = lax.broadcasted_iota(jnp.int32, (1, LANES), 1) < SSM_HEAD_DIM

    pairs_per_group = N_SSM_PAIRS // N_GROUPS
    for g in range(N_GROUPS):
        b_g = xc_ref[:, D_INNER + g * D_STATE:D_INNER + (g + 1) * D_STATE]
        c_g = xc_ref[:, D_INNER + BC_DIM + g * D_STATE:D_INNER + BC_DIM + (g + 1) * D_STATE]
        cb = _dot_nt(c_g.astype(BF16), b_g.astype(BF16))
        b_gt = b_g.T
        ys = []
        for j in range(pairs_per_group):
            pi = g * pairs_per_group + j
            ps = slice(pi * LANES, (pi + 1) * LANES)
            xs = xc_ref[:, ps]
            xs_b = xs.astype(BF16)
            st = st_ref[pi]
            rhs = jnp.concatenate([xs_b, st.astype(BF16)], axis=0)
            y_h, st_h = [], []
            for t in range(2):
                h = 2 * pi + t
                diff = a_cs[:, h:h + 1] - a_cs_t[h:h + 1, :]
                dec = jnp.exp(jnp.where(causal, diff, -jnp.inf))
                m_h = cb * dec * dt_t[h:h + 1, :]
                ce = c_g * e_cs[:, h:h + 1]
                lhs = jnp.concatenate([m_h, ce], axis=1).astype(BF16)
                y_h.append(_dot(lhs, rhs))
                s_new = _dot((b_gt * w_t[h:h + 1, :]).astype(BF16), xs_b)
                st_h.append(st * e_cs[BLK - 1:BLK, h:h + 1] + s_new)
            st_ref[pi] = jnp.where(lo, st_h[0], st_h[1])
            ys.append(jnp.where(lo, y_h[0], y_h[1]) + dskip_ref[:, ps] * xs)
        gs = slice(g * (D_INNER // N_GROUPS), (g + 1) * (D_INNER // N_GROUPS))
        zg = z_ref[:, gs].astype(F32)
        yg = jnp.concatenate(ys, axis=1) * (zg * _sigmoid(zg))
        ms = jnp.mean(yg * yg, axis=-1, keepdims=True)
        y_ref[:, gs] = (yg * lax.rsqrt(ms + EPS) * ng_ref[:, gs]).astype(BF16)

    @pl.when(c == pl.num_programs(1) - 1)
    def _():
        hfin_ref[0] = st_ref[...]


def _ssd(xbc, z, dtraw, tail0, h0, convw, convb, dtb, alog, dskip, ng, bsz, nc, n_pad):
    cur = lambda b, c: (b * nc + c, 0)
    return pl.pallas_call(
        functools.partial(_ssd_kernel, n_pad=n_pad),
        grid=(bsz, nc),
        in_specs=[
            pl.BlockSpec((BLK, CONV_DIM), cur), pl.BlockSpec((BLK, D_INNER), cur),
            pl.BlockSpec((BLK, LANES), cur),
            _const_spec((TAIL, CONV_DIM)), _const_spec((N_SSM_PAIRS, D_STATE, LANES)),
            _const_spec((CONV_WIDTH, CONV_DIM)), _const_spec((1, CONV_DIM)),
            _const_spec((N_SSM_HEADS, BLK)), _const_spec((N_SSM_HEADS, BLK)),
            _const_spec((1, D_INNER)), _const_spec((1, D_INNER)),
        ],
        out_specs=[pl.BlockSpec((BLK, D_INNER), cur),
                   pl.BlockSpec((1, N_SSM_PAIRS, D_STATE, LANES), lambda b, c: (b, 0, 0, 0))],
        out_shape=[jax.ShapeDtypeStruct((bsz * nc * BLK, D_INNER), BF16),
                   jax.ShapeDtypeStruct((bsz, N_SSM_PAIRS, D_STATE, LANES), F32)],
        scratch_shapes=[pltpu.VMEM((TAIL + BLK, CONV_DIM), BF16), pltpu.VMEM((BLK, CONV_DIM), F32),
                        pltpu.VMEM((N_SSM_PAIRS, D_STATE, LANES), F32)],
        compiler_params=pltpu.CompilerParams(dimension_semantics=("parallel", "arbitrary"),
                                             vmem_limit_bytes=VMEM_LIMIT),
        name="ssd_mixer",
    )(xbc, z, dtraw, tail0, h0, convw, convb, dtb, alog, dskip, ng)


def _merge_kernel(ya_ref, ys_ref, gl_ref, x_ref, wa_ref, ws_ref, wo_ref, g_ref, wr_ref, br_ref,
                  h_ref, u_ref, idx_ref, tw_ref):
    g0 = _sigmoid(gl_ref[:, :D_MODEL].astype(F32))
    g1 = _sigmoid(gl_ref[:, D_MODEL:].astype(F32))
    merged = g0 * _dot(ya_ref[...], wa_ref[...]) + g1 * _dot(ys_ref[...], ws_ref[...])
    h = x_ref[...] + _dot(merged.astype(BF16), wo_ref[...])
    h_ref[...] = h
    ms = jnp.mean(h * h, axis=-1, keepdims=True)
    u = (h * lax.rsqrt(ms + EPS) * g_ref[...]).astype(BF16)
    u_ref[...] = u
    logits = _dot(u, wr_ref[...]) + br_ref[...]
    lane = lax.broadcasted_iota(jnp.int32, logits.shape, 1).astype(F32)
    idx_out = jnp.zeros(logits.shape, F32)
    val_out = jnp.zeros(logits.shape, F32)
    top = None
    den = None
    for k in range(TOP_K):
        m = jnp.max(logits, axis=-1, keepdims=True)
        sel = jnp.min(jnp.where(logits == m, lane, float(LANES)), axis=-1, keepdims=True)
        if k == 0:
            top = m
        e = jnp.exp(m - top)
        den = e if k == 0 else den + e
        idx_out = jnp.where(lane == k, sel, idx_out)
        val_out = jnp.where(lane == k, e, val_out)
        logits = jnp.where(lane == sel, -jnp.inf, logits)
    idx_ref[...] = idx_out.astype(jnp.int32)
    tw_ref[...] = val_out / den


def _merge(ya, ys, gl, x2d, wa, ws, wo, g, wr, br, tm):
    rows = x2d.shape[0]
    row = lambda i: (i, 0)
    return pl.pallas_call(
        _merge_kernel,
        grid=(rows // tm,),
        in_specs=[pl.BlockSpec((tm, Q_DIM), row), pl.BlockSpec((tm, D_INNER), row),
                  pl.BlockSpec((tm, 2 * D_MODEL), row), pl.BlockSpec((tm, D_MODEL), row),
                  _const_spec((Q_DIM, D_MODEL)), _const_spec((D_INNER, D_MODEL)),
                  _const_spec((D_MODEL, D_MODEL)), _const_spec((1, D_MODEL)),
                  _const_spec((D_MODEL, LANES)), _const_spec((1, LANES))],
        out_specs=[pl.BlockSpec((tm, D_MODEL), row), pl.BlockSpec((tm, D_MODEL), row),
                   pl.BlockSpec((tm, LANES), row), pl.BlockSpec((tm, LANES), row)],
        out_shape=[jax.ShapeDtypeStruct((rows, D_MODEL), F32),
                   jax.ShapeDtypeStruct((rows, D_MODEL), BF16),
                   jax.ShapeDtypeStruct((rows, LANES), jnp.int32),
                   jax.ShapeDtypeStruct((rows, LANES), F32)],
        compiler_params=pltpu.CompilerParams(dimension_semantics=("parallel",),
                                             vmem_limit_bytes=VMEM_LIMIT),
        name="merge_router",
    )(ya, ys, gl, x2d, wa, ws, wo, g, wr, br)


def _route_kernel(idx_ref, dest_ref, info_ref, cnt_ref, base_ref, *, tm_e):
    p = pl.program_id(0)
    i = pl.program_id(1)
    tr = idx_ref.shape[0]
    lane = lax.broadcasted_iota(jnp.int32, (tr, LANES), 1)
    idx = idx_ref[...]
    onehot = [jnp.where(lane == idx[:, k:k + 1], 1.0, 0.0) for k in range(TOP_K)]
    oh_all = onehot[0] + onehot[1] + onehot[2] + onehot[3]
    tile_cnt = jnp.sum(oh_all, axis=0, keepdims=True)

    @pl.when(jnp.logical_and(p == 0, i == 0))
    def _():
        cnt_ref[...] = jnp.zeros_like(cnt_ref)

    @pl.when(p == 0)
    def _():
        cnt_ref[...] += tile_cnt

    @pl.when(jnp.logical_and(p == 1, i == 0))
    def _():
        counts = jnp.broadcast_to(cnt_ref[...], (8, LANES))
        padded = jnp.floor((counts + (tm_e - 1)) * (1.0 / tm_e)) * tm_e
        lane8 = lax.broadcasted_iota(jnp.int32, (8, LANES), 1)
        incl = padded
        shift = 1
        while shift < LANES:
            incl = incl + jnp.where(lane8 >= shift, pltpu.roll(incl, shift, 1), 0.0)
            shift *= 2
        pstart = incl - padded
        base_ref[...] = pstart[0:1]
        row8 = lax.broadcasted_iota(jnp.int32, (8, LANES), 0)
        info = jnp.where(row8 == 0, counts, jnp.where(row8 == 1, pstart, incl))
        info_ref[...] = info.astype(jnp.int32)

    @pl.when(p == 1)
    def _():
        r = lax.broadcasted_iota(jnp.int32, (tr, tr), 0)
        c = lax.broadcasted_iota(jnp.int32, (tr, tr), 1)
        before = jnp.where(r > c, 1.0, 0.0).astype(BF16)
        earlier = _dot(before, oh_all.astype(BF16))
        rank = base_ref[...] + earlier
        dest = jnp.zeros((tr, LANES), F32)
        for k in range(TOP_K):
            d_k = jnp.sum(onehot[k] * rank, axis=-1, keepdims=True)
            dest = jnp.where(lane == k, d_k, dest)
        dest_ref[...] = dest.T[0:8].astype(jnp.int32)
        base_ref[...] += tile_cnt


def _route(top_idx, tm_e, tr):
    rows = top_idx.shape[0]
    return pl.pallas_call(
        functools.partial(_route_kernel, tm_e=tm_e),
        grid=(2, rows // tr),
        in_specs=[pl.BlockSpec((tr, LANES), lambda p, i: (i, 0))],
        out_specs=[pl.BlockSpec((8, tr), lambda p, i: (0, i * p)),
                   pl.BlockSpec((8, LANES), lambda p, i: (0, 0))],
        out_shape=[jax.ShapeDtypeStruct((8, rows), jnp.int32),
                   jax.ShapeDtypeStruct((8, LANES), jnp.int32)],
        scratch_shapes=[pltpu.VMEM((1, LANES), F32), pltpu.VMEM((1, LANES), F32)],
        compiler_params=pltpu.CompilerParams(dimension_semantics=("arbitrary", "arbitrary"),
                                             vmem_limit_bytes=VMEM_LIMIT),
        name="route",
    )(top_idx)


FF_CHUNK = 2 * LANES


def _moe_kernel(te_ref, nv_ref, x_ref, w1_ref, b1_ref, w2_ref, b2_ref, y_ref, w1s_ref, w2s_ref):
    i = pl.program_id(0)
    valid = i < nv_ref[0]
    fresh = jnp.logical_or(i == 0, te_ref[i] != te_ref[jnp.maximum(i - 1, 0)])

    @pl.when(jnp.logical_and(valid, fresh))
    def _():
        r = lax.broadcasted_iota(jnp.int32, (FF_CHUNK, FF_CHUNK), 0)
        c = lax.broadcasted_iota(jnp.int32, (FF_CHUNK, FF_CHUNK), 1)
        src = jnp.where(c < LANES, 2 * c, 2 * (c - LANES) + 1)
        perm = jnp.where(r == src, 1.0, 0.0).astype(BF16)
        for c0 in range(0, 2 * D_FF, FF_CHUNK):
            cs = slice(c0, c0 + FF_CHUNK)
            w1s_ref[:, cs] = _dot(w1_ref[0, :, cs].astype(BF16), perm).astype(BF16)
        w2s_ref[...] = w2_ref[0].astype(BF16)

    @pl.when(valid)
    def _():
        x = x_ref[...]
        acts = []
        for c0 in range(0, 2 * D_FF, FF_CHUNK):
            cs = slice(c0, c0 + FF_CHUNK)
            hh = _dot(x, w1s_ref[:, cs]) + b1_ref[0, :, cs]
            glu = jnp.minimum(hh[:, :LANES], SWIGLU_LIMIT)
            lin = jnp.clip(hh[:, LANES:], -SWIGLU_LIMIT, SWIGLU_LIMIT)
            acts.append((glu * _sigmoid(SWIGLU_ALPHA * glu) * (lin + 1.0)).astype(BF16))
        act = jnp.concatenate(acts, axis=1)
        y_ref[...] = (_dot(act, w2s_ref[...]) + b2_ref[0]).astype(BF16)

    @pl.when(jnp.logical_not(valid))
    def _():
        y_ref[...] = jnp.zeros_like(y_ref)


def _moe(tile_e, n_valid, xrows, w1, b1, w2, b2, tm):
    n_tiles = xrows.shape[0] // tm
    wsel = lambda i, te, nv: (te[i], 0, 0)
    grid_spec = pltpu.PrefetchScalarGridSpec(
        num_scalar_prefetch=2,
        grid=(n_tiles,),
        in_specs=[pl.BlockSpec((tm, D_MODEL), lambda i, te, nv: (i, 0)),
                  pl.BlockSpec((1, D_MODEL, 2 * D_FF), wsel), pl.BlockSpec((1, 1, 2 * D_FF), wsel),
                  pl.BlockSpec((1, D_FF, D_MODEL), wsel), pl.BlockSpec((1, 1, D_MODEL), wsel)],
        out_specs=pl.BlockSpec((tm, D_MODEL), lambda i, te, nv: (i, 0)),
        scratch_shapes=[pltpu.VMEM((D_MODEL, 2 * D_FF), BF16), pltpu.VMEM((D_FF, D_MODEL), BF16)],
    )
    return pl.pallas_call(
        _moe_kernel,
        grid_spec=grid_spec,
        out_shape=jax.ShapeDtypeStruct(xrows.shape, BF16),
        compiler_params=pltpu.CompilerParams(dimension_semantics=("arbitrary",),
                                             vmem_limit_bytes=VMEM_LIMIT),
        name="moe_experts",
    )(tile_e, n_valid, xrows, w1, b1, w2, b2)


def _final_kernel(h_ref, y0_ref, y1_ref, y2_ref, y3_ref, tw_ref, g_ref, o_ref):
    h = h_ref[...]
    tw = tw_ref[...]
    for k, y_ref in enumerate((y0_ref, y1_ref, y2_ref, y3_ref)):
        h = h + tw[:, k:k + 1] * y_ref[...].astype(F32)
    ms = jnp.mean(h * h, axis=-1, keepdims=True)
    o_ref[...] = h * lax.rsqrt(ms + EPS) * g_ref[...]


def _final(h, yg, tw, g, tm):
    rows = h.shape[0]
    row = lambda i: (i, 0)
    return pl.pallas_call(
        _final_kernel,
        grid=(rows // tm,),
        in_specs=[pl.BlockSpec((tm, D_MODEL), row)] * (1 + TOP_K)
        + [pl.BlockSpec((tm, LANES), row), _const_spec((1, D_MODEL))],
        out_specs=pl.BlockSpec((tm, D_MODEL), row),
        out_shape=jax.ShapeDtypeStruct((rows, D_MODEL), F32),
        compiler_params=pltpu.CompilerParams(dimension_semantics=("parallel",),
                                             vmem_limit_bytes=VMEM_LIMIT),
        name="combine_final_norm",
    )(h, *yg, tw, g)


def _t5_bucket(dist):
    n = jnp.maximum(dist, 0)
    max_exact = N_BUCKETS // 2
    nf = jnp.maximum(n, 1).astype(F32)
    large = max_exact + (jnp.log(nf / max_exact) / math.log(MAX_DISTANCE / max_exact)
                         * (N_BUCKETS - max_exact)).astype(jnp.int32)
    large = jnp.minimum(large, N_BUCKETS - 1)
    return jnp.where(n < max_exact, n, large)


def _bias_tables(rel_bias, nb):
    table = rel_bias.astype(F32)
    j = jnp.arange(BLK, dtype=jnp.int32)[:, None]
    s = jnp.arange(BLK, dtype=jnp.int32)[None, :]
    dist = jnp.concatenate([j + BLK - s, j - s], axis=1)
    valid = (dist >= 0) & (dist < WINDOW)
    def lookup(bucket):
        onehot = (bucket[..., None] == jnp.arange(N_BUCKETS, dtype=jnp.int32)).astype(F32)
        return jnp.einsum('...b,bh->...h', onehot, table, precision=lax.Precision.HIGHEST)

    main = jnp.where(valid[:, :, None], lookup(_t5_bucket(dist)), NEG)
    n_idx = jnp.arange(nb, dtype=jnp.int32)[:, None, None]
    m = jnp.arange(N_META, dtype=jnp.int32)[None, None, :]
    dist_m = N_META + n_idx * BLK + j[None] - m
    meta = lookup(_t5_bucket(dist_m))
    meta = jnp.pad(meta, ((0, 0), (0, 0), (0, LANES - N_META), (0, 0)), constant_values=NEG)
    return main.transpose(2, 0, 1), meta.transpose(0, 3, 1, 2)


def _dup_heads(w):
    w = w.reshape(w.shape[0], N_KV_HEADS, 1, HEAD_DIM)
    return jnp.broadcast_to(w, (w.shape[0], N_KV_HEADS, 2, HEAD_DIM)).reshape(w.shape[0], KV_DUP)


def _pad_lanes(v, value=0.0):
    v = v.reshape(1, -1).astype(F32)
    return jnp.pad(v, ((0, 0), (0, LANES - v.shape[1])), constant_values=value)


def _head_rows(v):
    return jnp.broadcast_to(v.astype(F32)[:, None], (N_SSM_HEADS, BLK))


def _row_tile(rows, target):
    tm = min(rows, target)
    assert rows % tm == 0
    return tm


def kernel(x, meta_tokens, rel_bias, norm_mix_g, w_in, attn_sinks, conv_w, conv_b, dt_bias, a_log,
           d_skip, ssm_norm_g, w_attn_br, w_ssm_br, w_out, norm_ffn_g, w_router, b_router,
           w_exp_in, b_exp_in, w_exp_out, b_exp_out, norm_final_g):
    bsz, seq, _ = x.shape
    assert seq % BLK == 0 and w_in.shape[0] == 1
    nb = seq // BLK
    rows = bsz * seq
    x2d = x.reshape(rows, D_MODEL)

    wi = w_in[0]
    o_k, o_v, o_z = Q_DIM, Q_DIM + KV_DIM, Q_DIM + 2 * KV_DIM
    o_x, o_dt = o_z + D_INNER, o_z + D_INNER + CONV_DIM
    o_g = o_dt + N_SSM_HEADS
    w_proj = jnp.concatenate([
        wi[:, :o_k] * (HEAD_DIM ** -0.5), _dup_heads(wi[:, o_k:o_v]), _dup_heads(wi[:, o_v:o_z]),
        wi[:, o_z:o_x], wi[:, o_x:o_dt], wi[:, o_g:]], axis=1).astype(BF16)
    w_dt = jnp.pad(wi[:, o_dt:o_g], ((0, 0), (0, LANES - N_SSM_HEADS))).astype(BF16)
    g_mix = norm_mix_g[0].reshape(1, D_MODEL)

    tm = _row_tile(rows, 512)
    q, k, v, z, xbc, gate, dtraw = _inproj(x2d, g_mix, w_proj, w_dt, tm)
    _, k_m, v_m, z_m, xbc_m, _, dtraw_m = _inproj(meta_tokens.astype(F32), g_mix, w_proj, w_dt,
                                                  N_META)

    pad_meta = ((BLK - N_META, 0), (0, 0))
    pad_meta_tail = ((0, BLK - N_META), (0, 0))
    bias_main, bias_meta = _bias_tables(rel_bias, nb)
    y_attn = _attention(q, k, v, jnp.pad(k_m, pad_meta_tail), jnp.pad(v_m, pad_meta_tail),
                        bias_main, bias_meta, attn_sinks[0].astype(F32), bsz, nb)

    ssd_params = (conv_w[0].astype(F32), conv_b[0].reshape(1, CONV_DIM).astype(F32),
                  _head_rows(dt_bias[0]), _head_rows(a_log[0]),
                  jnp.repeat(d_skip[0].astype(F32), SSM_HEAD_DIM).reshape(1, D_INNER),
                  ssm_norm_g[0].reshape(1, D_INNER).astype(F32))
    xbc_mp = jnp.pad(xbc_m, pad_meta)
    zero_state = jnp.zeros((N_SSM_PAIRS, D_STATE, LANES), F32)
    _, h_meta = _ssd(xbc_mp, jnp.pad(z_m, pad_meta), jnp.pad(dtraw_m, pad_meta),
                     jnp.zeros((TAIL, CONV_DIM), BF16), zero_state, *ssd_params, 1, 1,
                     BLK - N_META)
    y_ssm, _ = _ssd(xbc, z, dtraw, xbc_mp[BLK - TAIL:], h_meta[0], *ssd_params, bsz, nb, 0)

    w_r = jnp.pad(w_router[0], ((0, 0), (0, LANES - N_EXPERTS))).astype(BF16)
    b_r = _pad_lanes(b_router[0], NEG)
    h1, u2, top_idx, top_w = _merge(
        y_attn, y_ssm, gate, x2d, w_attn_br[0].astype(BF16), w_ssm_br[0].astype(BF16),
        w_out[0].astype(BF16), norm_ffn_g[0].reshape(1, D_MODEL), w_r, b_r, tm)

    tm_e = 512
    n_tiles = -(-(rows * TOP_K + N_EXPERTS * (tm_e - 1)) // tm_e)
    dest_t, info = _route(top_idx, tm_e, tm)
    pend = info[2, :N_EXPERTS]
    row_tok = (jnp.arange(n_tiles * tm_e, dtype=jnp.int32) % rows).at[
        dest_t[:TOP_K].reshape(-1)].set(jnp.tile(jnp.arange(rows, dtype=jnp.int32), TOP_K),
                                        unique_indices=True)
    tile_start = jnp.arange(n_tiles, dtype=jnp.int32) * tm_e
    tile_e = jnp.minimum(jnp.sum(tile_start[:, None] >= pend[None, :], axis=1),
                         N_EXPERTS - 1).astype(jnp.int32)
    n_valid = pend[-1:] // tm_e

    b1 = b_exp_in[0].astype(F32).reshape(N_EXPERTS, 2 * D_FF // FF_CHUNK, LANES, 2)
    b1 = b1.transpose(0, 1, 3, 2).reshape(N_EXPERTS, 1, 2 * D_FF)
    y_rows = _moe(tile_e, n_valid, u2[row_tok], w_exp_in[0], b1, w_exp_out[0],
                  b_exp_out[0].reshape(N_EXPERTS, 1, D_MODEL).astype(F32), tm_e)

    yg = [y_rows[dest_t[k]] for k in range(TOP_K)]
    out = _final(h1, yg, top_w, norm_final_g.reshape(1, D_MODEL).astype(F32), tm)
    return out.reshape(bsz, seq, D_MODEL)
```

```python
import functools
import math

import jax
import jax.numpy as jnp
from jax import lax
from jax.experimental import pallas as pl
from jax.experimental.pallas import tpu as pltpu

F32 = jnp.float32
BF16 = jnp.bfloat16

D_MODEL = 1024
N_META = 16
N_HEADS = 16
N_KV_HEADS = 4
HEAD_DIM = 64
Q_PER_KV = N_HEADS // N_KV_HEADS
Q_DIM = N_HEADS * HEAD_DIM
KV_DIM = N_KV_HEADS * HEAD_DIM
WINDOW = 128
BLK = 128
N_BUCKETS = 32
MAX_DISTANCE = 128
D_INNER = 2 * D_MODEL
SSM_HEAD_DIM = 64
N_SSM_HEADS = D_INNER // SSM_HEAD_DIM
N_GROUPS = 4
D_STATE = 128
CONV_WIDTH = 4
BC_DIM = N_GROUPS * D_STATE
CONV_DIM = D_INNER + 2 * BC_DIM
N_EXPERTS = 32
TOP_K = 4
D_FF = D_MODEL
SWIGLU_ALPHA = 1.702
SWIGLU_LIMIT = 7.0
EPS = 1e-5
NEG = -1e30

LANES = 128
KV_DUP = 2 * KV_DIM
N_PAIRS = N_HEADS // 2
N_SSM_PAIRS = N_SSM_HEADS // 2
CTX = 3 * BLK
VMEM_LIMIT = 56 * 1024 * 1024

PROJ_SEGS = (("q", Q_DIM), ("k", KV_DUP), ("v", KV_DUP), ("z", D_INNER), ("xbc", CONV_DIM),
             ("gate", 2 * D_MODEL))
PROJ_N = sum(w for _, w in PROJ_SEGS)
PROJ_CHUNK = 512


def _dot(a, b):
    return jnp.dot(a, b, preferred_element_type=F32)


def _dot_nt(a, b):
    return lax.dot_general(a, b, (((1,), (1,)), ((), ())), preferred_element_type=F32)


def _sigmoid(x):
    return 1.0 / (1.0 + jnp.exp(-x))


def _const_spec(shape):
    nd = len(shape)
    return pl.BlockSpec(shape, lambda *_: (0,) * nd, pipeline_mode=pl.Buffered(1))


def _inproj_kernel(x_ref, g_ref, w_ref, wdt_ref, q_ref, k_ref, v_ref, z_ref, xbc_ref, gate_ref,
                   dt_ref):
    x = x_ref[...]
    ms = jnp.mean(x * x, axis=-1, keepdims=True)
    u = (x * lax.rsqrt(ms + EPS) * g_ref[...]).astype(BF16)
    outs = (q_ref, k_ref, v_ref, z_ref, xbc_ref, gate_ref)
    off = 0
    for ref, (_, width) in zip(outs, PROJ_SEGS):
        for c0 in range(0, width, PROJ_CHUNK):
            ref[:, c0:c0 + PROJ_CHUNK] = _dot(
                u, w_ref[:, off + c0:off + c0 + PROJ_CHUNK]).astype(BF16)
        off += width
    dt_ref[...] = _dot(u, wdt_ref[...])


def _inproj(x2d, g, w, wdt, tm):
    rows = x2d.shape[0]
    row = lambda i: (i, 0)
    out_shape = [jax.ShapeDtypeStruct((rows, wd), BF16) for _, wd in PROJ_SEGS]
    out_shape.append(jax.ShapeDtypeStruct((rows, LANES), F32))
    out_specs = [pl.BlockSpec((tm, wd), row) for _, wd in PROJ_SEGS]
    out_specs.append(pl.BlockSpec((tm, LANES), row))
    return pl.pallas_call(
        _inproj_kernel,
        grid=(rows // tm,),
        in_specs=[pl.BlockSpec((tm, D_MODEL), row), _const_spec((1, D_MODEL)),
                  _const_spec((D_MODEL, PROJ_N)), _const_spec((D_MODEL, LANES))],
        out_specs=out_specs,
        out_shape=out_shape,
        compiler_params=pltpu.CompilerParams(dimension_semantics=("parallel",),
                                             vmem_limit_bytes=VMEM_LIMIT),
        name="inproj",
    )(x2d, g, w, wdt)


def _attn_kernel(sink_ref, q_ref, kp_ref, kc_ref, km_ref, vp_ref, vc_ref, vm_ref, bmain_ref,
                 bmeta_ref, o_ref):
    n = pl.program_id(1)
    lane_ctx = lax.broadcasted_iota(jnp.int32, (1, CTX), 1)
    pen = jnp.where(jnp.logical_and(n == 0, lane_ctx < BLK), NEG, 0.0).astype(F32)
    lo = lax.broadcasted_iota(jnp.int32, (1, LANES), 1) < HEAD_DIM
    zero = jnp.zeros((), BF16)
    for g in range(N_KV_HEADS):
        cs = slice(g * LANES, (g + 1) * LANES)
        kk = jnp.concatenate([kp_ref[:, cs], kc_ref[:, cs], km_ref[:, cs]], axis=0)
        vv = jnp.concatenate([vp_ref[:, cs], vc_ref[:, cs], vm_ref[:, cs]], axis=0)
        k_lo = jnp.where(lo, kk, zero)
        k_hi = jnp.where(lo, zero, kk)
        v_bd = jnp.concatenate([jnp.where(lo, vv, zero), jnp.where(lo, zero, vv)], axis=0)
        for j in range(Q_PER_KV // 2):
            pi = g * (Q_PER_KV // 2) + j
            qp = q_ref[:, pi * LANES:(pi + 1) * LANES]
            ps, invs = [], []
            for t, kx in enumerate((k_lo, k_hi)):
                h = 2 * pi + t
                bias = jnp.concatenate([bmain_ref[h], bmeta_ref[0, h]], axis=1) + pen
                s = _dot_nt(qp, kx) + bias
                sink = sink_ref[h]
                m = jnp.maximum(jnp.max(s, axis=-1, keepdims=True), sink)
                p = jnp.exp(s - m)
                den = jnp.sum(p, axis=-1, keepdims=True) + jnp.exp(sink - m)
                ps.append(p.astype(BF16))
                invs.append(1.0 / den)
            o = _dot(jnp.concatenate(ps, axis=1), v_bd)
            o = o * jnp.where(lo, invs[0], invs[1])
            o_ref[:, pi * LANES:(pi + 1) * LANES] = o.astype(BF16)


def _attention(q, k, v, k_meta, v_meta, bias_main, bias_meta, sinks, bsz, nb):
    cur = lambda b, n, *_: (b * nb + n, 0)
    prev = lambda b, n, *_: (jnp.maximum(b * nb + n - 1, 0), 0)
    const2 = lambda b, n, *_: (0, 0)
    grid_spec = pltpu.PrefetchScalarGridSpec(
        num_scalar_prefetch=1,
        grid=(bsz, nb),
        in_specs=[
            pl.BlockSpec((BLK, Q_DIM), cur),
            pl.BlockSpec((BLK, KV_DUP), prev), pl.BlockSpec((BLK, KV_DUP), cur),
            pl.BlockSpec((BLK, KV_DUP), const2),
            pl.BlockSpec((BLK, KV_DUP), prev), pl.BlockSpec((BLK, KV_DUP), cur),
            pl.BlockSpec((BLK, KV_DUP), const2),
            pl.BlockSpec((N_HEADS, BLK, 2 * BLK), lambda b, n, *_: (0, 0, 0)),
            pl.BlockSpec((1, N_HEADS, BLK, LANES), lambda b, n, *_: (n, 0, 0, 0)),
        ],
        out_specs=pl.BlockSpec((BLK, Q_DIM), cur),
    )
    return pl.pallas_call(
        _attn_kernel,
        grid_spec=grid_spec,
        out_shape=jax.ShapeDtypeStruct((bsz * nb * BLK, Q_DIM), BF16),
        compiler_params=pltpu.CompilerParams(dimension_semantics=("parallel", "arbitrary"),
                                             vmem_limit_bytes=VMEM_LIMIT),
        name="swa_attention",
    )(sinks, q, k, k, k_meta, v, v, v_meta, bias_main, bias_meta)


CONV_COLS = 256
TAIL = 16


def _ssd_kernel(xbc_ref, z_ref, dtraw_ref, tail0_ref, h0_ref, convw_ref, convb_ref, dtb_ref,
                alog_ref, dskip_ref, ng_ref, y_ref, hfin_ref, win_ref, xc_ref, st_ref, *, n_pad):
    for s in range(xbc_ref.shape[0]):
        _ssd_chunk(xbc_ref.at[s], z_ref.at[s], dtraw_ref.at[s], tail0_ref, h0_ref, convw_ref,
                   convb_ref, dtb_ref, alog_ref, dskip_ref, ng_ref, y_ref.at[s], hfin_ref.at[s],
                   win_ref.at[s], xc_ref.at[s], st_ref.at[s], n_pad)


def _ssd_chunk(xbc_ref, z_ref, dtraw_ref, tail0_ref, h0_ref, convw_ref, convb_ref, dtb_ref,
               alog_ref, dskip_ref, ng_ref, y_ref, hfin_ref, win_ref, xc_ref, st_ref, n_pad):
    c = pl.program_id(1)

    @pl.when(c == 0)
    def _():
        win_ref[0:TAIL, :] = tail0_ref[...]
        st_ref[...] = h0_ref[...]

    win_ref[TAIL:TAIL + BLK, :] = xbc_ref[...]
    sr = lax.broadcasted_iota(jnp.int32, (BLK, TAIL + BLK), 0)
    sc = lax.broadcasted_iota(jnp.int32, (BLK, TAIL + BLK), 1)
    shifts = [jnp.where(sc == sr + (TAIL - (CONV_WIDTH - 1) + w), 1.0, 0.0).astype(BF16)
              for w in range(CONV_WIDTH - 1)]
    live = lax.broadcasted_iota(jnp.int32, (BLK, 1), 0) >= n_pad
    for c0 in range(0, CONV_DIM, CONV_COLS):
        cs = slice(c0, c0 + CONV_COLS)
        wv = win_ref[:, cs]
        acc = convb_ref[:, cs] + convw_ref[CONV_WIDTH - 1:CONV_WIDTH, cs] * wv[TAIL:].astype(F32)
        for w in range(CONV_WIDTH - 1):
            acc = acc + convw_ref[w:w + 1, cs] * _dot(shifts[w], wv)
        act = acc * _sigmoid(acc)
        if n_pad:
            act = jnp.where(live, act, 0.0)
        xc_ref[:, cs] = act
    win_ref[0:TAIL, :] = win_ref[BLK:BLK + TAIL, :]

    lane_t = lax.broadcasted_iota(jnp.int32, (N_SSM_HEADS, BLK), 1)
    x_t = dtraw_ref[...].T[:N_SSM_HEADS] + dtb_ref[...]
    dt_t = jnp.maximum(x_t, 0.0) + jnp.log1p(jnp.exp(-jnp.abs(x_t)))
    if n_pad:
        dt_t = jnp.where(lane_t >= n_pad, dt_t, 0.0)
    a_cs_t = dt_t * (-jnp.exp(alog_ref[...]))
    shift = 1
    while shift < BLK:
        a_cs_t = a_cs_t + jnp.where(lane_t >= shift, pltpu.roll(a_cs_t, shift, 1), 0.0)
        shift *= 2
    w_t = dt_t * jnp.exp(a_cs_t[:, BLK - 1:BLK] - a_cs_t)
    src_t = a_cs_t - jnp.log(dt_t)
    fill = jnp.zeros((BLK - N_SSM_HEADS, BLK), F32)
    a_cs = jnp.concatenate([a_cs_t, fill], axis=0).T
    e_cs = jnp.exp(a_cs)
    ri = lax.broadcasted_iota(jnp.int32, (BLK, BLK), 0)
    ci = lax.broadcasted_iota(jnp.int32, (BLK, BLK), 1)
    causal = ri >= ci
    lo = lax.broadcasted_iota(jnp.int32, (1, LANES), 1) < SSM_HEAD_DIM

    pairs_per_group = N_SSM_PAIRS // N_GROUPS
    for g in range(N_GROUPS):
        b_g = xc_ref[:, D_INNER + g * D_STATE:D_INNER + (g + 1) * D_STATE]
        c_g = xc_ref[:, D_INNER + BC_DIM + g * D_STATE:D_INNER + BC_DIM + (g + 1) * D_STATE]
        cb = _dot_nt(c_g.astype(BF16), b_g.astype(BF16))
        b_gt = b_g.T
        ys = []
        for j in range(pairs_per_group):
            pi = g * pairs_per_group + j
            ps = slice(pi * LANES, (pi + 1) * LANES)
            xs = xc_ref[:, ps]
            xs_b = xs.astype(BF16)
            st = st_ref[pi]
            rhs = jnp.concatenate([xs_b, st.astype(BF16)], axis=0)
            y_h, st_h = [], []
            for t in range(2):
                h = 2 * pi + t
                diff = a_cs[:, h:h + 1] - src_t[h:h + 1, :]
                m_h = cb * jnp.exp(jnp.where(causal, diff, -jnp.inf))
                ce = c_g * e_cs[:, h:h + 1]
                lhs = jnp.concatenate([m_h, ce], axis=1).astype(BF16)
                y_h.append(_dot(lhs, rhs))
                s_new = _dot((b_gt * w_t[h:h + 1, :]).astype(BF16), xs_b)
                st_h.append(st * e_cs[BLK - 1:BLK, h:h + 1] + s_new)
            st_ref[pi] = jnp.where(lo, st_h[0], st_h[1])
            ys.append(jnp.where(lo, y_h[0], y_h[1]) + dskip_ref[:, ps] * xs)
        gs = slice(g * (D_INNER // N_GROUPS), (g + 1) * (D_INNER // N_GROUPS))
        zg = z_ref[:, gs].astype(F32)
        yg = jnp.concatenate(ys, axis=1) * (zg * _sigmoid(zg))
        ms = jnp.mean(yg * yg, axis=-1, keepdims=True)
        y_ref[:, gs] = (yg * lax.rsqrt(ms + EPS) * ng_ref[:, gs]).astype(BF16)

    @pl.when(c == pl.num_programs(1) - 1)
    def _():
        hfin_ref[0] = st_ref[...]


def _ssd(xbc, z, dtraw, tail0, h0, convw, convb, dtb, alog, dskip, ng, bsz, nc, n_pad):
    n_seq = 2 if bsz % 2 == 0 else 1
    nbg = bsz // n_seq
    rows_g = nbg * nc * BLK
    cur = lambda b, c: (0, b * nc + c, 0)
    split = lambda a: a.reshape(n_seq, rows_g, a.shape[-1])
    y, h_fin = pl.pallas_call(
        functools.partial(_ssd_kernel, n_pad=n_pad),
        grid=(nbg, nc),
        in_specs=[
            pl.BlockSpec((n_seq, BLK, CONV_DIM), cur), pl.BlockSpec((n_seq, BLK, D_INNER), cur),
            pl.BlockSpec((n_seq, BLK, LANES), cur),
            _const_spec((TAIL, CONV_DIM)), _const_spec((N_SSM_PAIRS, D_STATE, LANES)),
            _const_spec((CONV_WIDTH, CONV_DIM)), _const_spec((1, CONV_DIM)),
            _const_spec((N_SSM_HEADS, BLK)), _const_spec((N_SSM_HEADS, BLK)),
            _const_spec((1, D_INNER)), _const_spec((1, D_INNER)),
        ],
        out_specs=[pl.BlockSpec((n_seq, BLK, D_INNER), cur),
                   pl.BlockSpec((n_seq, 1, N_SSM_PAIRS, D_STATE, LANES),
                                lambda b, c: (0, b, 0, 0, 0))],
        out_shape=[jax.ShapeDtypeStruct((n_seq, rows_g, D_INNER), BF16),
                   jax.ShapeDtypeStruct((n_seq, nbg, N_SSM_PAIRS, D_STATE, LANES), F32)],
        scratch_shapes=[pltpu.VMEM((n_seq, TAIL + BLK, CONV_DIM), BF16),
                        pltpu.VMEM((n_seq, BLK, CONV_DIM), F32),
                        pltpu.VMEM((n_seq, N_SSM_PAIRS, D_STATE, LANES), F32)],
        compiler_params=pltpu.CompilerParams(dimension_semantics=("parallel", "arbitrary"),
                                             vmem_limit_bytes=VMEM_LIMIT),
        name="ssd_mixer",
    )(split(xbc), split(z), split(dtraw), tail0, h0, convw, convb, dtb, alog, dskip, ng)
    return (y.reshape(bsz * nc * BLK, D_INNER),
            h_fin.reshape(bsz, N_SSM_PAIRS, D_STATE, LANES))


def _merge_kernel(ya_ref, ys_ref, gl_ref, x_ref, wa_ref, ws_ref, wo_ref, g_ref, wr_ref, br_ref,
                  h_ref, u_ref, idx_ref, tw_ref):
    g0 = _sigmoid(gl_ref[:, :D_MODEL].astype(F32))
    g1 = _sigmoid(gl_ref[:, D_MODEL:].astype(F32))
    merged = g0 * _dot(ya_ref[...], wa_ref[...]) + g1 * _dot(ys_ref[...], ws_ref[...])
    h = x_ref[...] + _dot(merged.astype(BF16), wo_ref[...])
    h_ref[...] = h
    ms = jnp.mean(h * h, axis=-1, keepdims=True)
    u = (h * lax.rsqrt(ms + EPS) * g_ref[...]).astype(BF16)
    u_ref[...] = u
    logits = _dot(u, wr_ref[...]) + br_ref[...]
    lane = lax.broadcasted_iota(jnp.int32, logits.shape, 1).astype(F32)
    idx_out = jnp.zeros(logits.shape, F32)
    val_out = jnp.zeros(logits.shape, F32)
    top = None
    den = None
    for k in range(TOP_K):
        m = jnp.max(logits, axis=-1, keepdims=True)
        sel = jnp.min(jnp.where(logits == m, lane, float(LANES)), axis=-1, keepdims=True)
        if k == 0:
            top = m
        e = jnp.exp(m - top)
        den = e if k == 0 else den + e
        idx_out = jnp.where(lane == k, sel, idx_out)
        val_out = jnp.where(lane == k, e, val_out)
        logits = jnp.where(lane == sel, -jnp.inf, logits)
    idx_ref[...] = idx_out.astype(jnp.int32)
    tw_ref[...] = val_out / den


def _merge(ya, ys, gl, x2d, wa, ws, wo, g, wr, br, tm):
    rows = x2d.shape[0]
    row = lambda i: (i, 0)
    return pl.pallas_call(
        _merge_kernel,
        grid=(rows // tm,),
        in_specs=[pl.BlockSpec((tm, Q_DIM), row), pl.BlockSpec((tm, D_INNER), row),
                  pl.BlockSpec((tm, 2 * D_MODEL), row), pl.BlockSpec((tm, D_MODEL), row),
                  _const_spec((Q_DIM, D_MODEL)), _const_spec((D_INNER, D_MODEL)),
                  _const_spec((D_MODEL, D_MODEL)), _const_spec((1, D_MODEL)),
                  _const_spec((D_MODEL, LANES)), _const_spec((1, LANES))],
        out_specs=[pl.BlockSpec((tm, D_MODEL), row), pl.BlockSpec((tm, D_MODEL), row),
                   pl.BlockSpec((tm, LANES), row), pl.BlockSpec((tm, LANES), row)],
        out_shape=[jax.ShapeDtypeStruct((rows, D_MODEL), F32),
                   jax.ShapeDtypeStruct((rows, D_MODEL), BF16),
                   jax.ShapeDtypeStruct((rows, LANES), jnp.int32),
                   jax.ShapeDtypeStruct((rows, LANES), F32)],
        compiler_params=pltpu.CompilerParams(dimension_semantics=("parallel",),
                                             vmem_limit_bytes=VMEM_LIMIT),
        name="merge_router",
    )(ya, ys, gl, x2d, wa, ws, wo, g, wr, br)


def _route_kernel(idx_ref, dest_ref, info_ref, cnt_ref, base_ref, *, tm_e):
    p = pl.program_id(0)
    i = pl.program_id(1)
    tr = idx_ref.shape[0]
    lane = lax.broadcasted_iota(jnp.int32, (tr, LANES), 1)
    idx = idx_ref[...]
    onehot = [jnp.where(lane == idx[:, k:k + 1], 1.0, 0.0) for k in range(TOP_K)]
    oh_all = onehot[0] + onehot[1] + onehot[2] + onehot[3]
    tile_cnt = jnp.sum(oh_all, axis=0, keepdims=True)

    @pl.when(jnp.logical_and(p == 0, i == 0))
    def _():
        cnt_ref[...] = jnp.zeros_like(cnt_ref)

    @pl.when(p == 0)
    def _():
        cnt_ref[...] += tile_cnt

    @pl.when(jnp.logical_and(p == 1, i == 0))
    def _():
        counts = jnp.broadcast_to(cnt_ref[...], (8, LANES))
        padded = jnp.floor((counts + (tm_e - 1)) * (1.0 / tm_e)) * tm_e
        lane8 = lax.broadcasted_iota(jnp.int32, (8, LANES), 1)
        incl = padded
        shift = 1
        while shift < LANES:
            incl = incl + jnp.where(lane8 >= shift, pltpu.roll(incl, shift, 1), 0.0)
            shift *= 2
        pstart = incl - padded
        base_ref[...] = pstart[0:1]
        row8 = lax.broadcasted_iota(jnp.int32, (8, LANES), 0)
        info = jnp.where(row8 == 0, counts, jnp.where(row8 == 1, pstart, incl))
        info_ref[...] = info.astype(jnp.int32)

    @pl.when(p == 1)
    def _():
        r = lax.broadcasted_iota(jnp.int32, (tr, tr), 0)
        c = lax.broadcasted_iota(jnp.int32, (tr, tr), 1)
        before = jnp.where(r > c, 1.0, 0.0).astype(BF16)
        earlier = _dot(before, oh_all.astype(BF16))
        rank = base_ref[...] + earlier
        dest = jnp.zeros((tr, LANES), F32)
        for k in range(TOP_K):
            d_k = jnp.sum(onehot[k] * rank, axis=-1, keepdims=True)
            dest = jnp.where(lane == k, d_k, dest)
        dest_ref[...] = dest.T[0:8].astype(jnp.int32)
        base_ref[...] += tile_cnt


def _route(top_idx, tm_e, tr):
    rows = top_idx.shape[0]
    return pl.pallas_call(
        functools.partial(_route_kernel, tm_e=tm_e),
        grid=(2, rows // tr),
        in_specs=[pl.BlockSpec((tr, LANES), lambda p, i: (i, 0))],
        out_specs=[pl.BlockSpec((8, tr), lambda p, i: (0, i * p)),
                   pl.BlockSpec((8, LANES), lambda p, i: (0, 0))],
        out_shape=[jax.ShapeDtypeStruct((8, rows), jnp.int32),
                   jax.ShapeDtypeStruct((8, LANES), jnp.int32)],
        scratch_shapes=[pltpu.VMEM((1, LANES), F32), pltpu.VMEM((1, LANES), F32)],
        compiler_params=pltpu.CompilerParams(dimension_semantics=("arbitrary", "arbitrary"),
                                             vmem_limit_bytes=VMEM_LIMIT),
        name="route",
    )(top_idx)


FF_CHUNK = 2 * LANES


def _moe_kernel(te_ref, nv_ref, x_ref, w1_ref, b1_ref, w2_ref, b2_ref, y_ref, w1s_ref, w2s_ref):
    i = pl.program_id(0)
    valid = i < nv_ref[0]
    fresh = jnp.logical_or(i == 0, te_ref[i] != te_ref[jnp.maximum(i - 1, 0)])

    @pl.when(jnp.logical_and(valid, fresh))
    def _():
        r = lax.broadcasted_iota(jnp.int32, (FF_CHUNK, FF_CHUNK), 0)
        c = lax.broadcasted_iota(jnp.int32, (FF_CHUNK, FF_CHUNK), 1)
        src = jnp.where(c < LANES, 2 * c, 2 * (c - LANES) + 1)
        perm = jnp.where(r == src, 1.0, 0.0).astype(BF16)
        for c0 in range(0, 2 * D_FF, FF_CHUNK):
            cs = slice(c0, c0 + FF_CHUNK)
            w1s_ref[:, cs] = _dot(w1_ref[0, :, cs].astype(BF16), perm).astype(BF16)
        w2s_ref[...] = w2_ref[0].astype(BF16)

    @pl.when(valid)
    def _():
        x = x_ref[...]
        acts = []
        for c0 in range(0, 2 * D_FF, FF_CHUNK):
            cs = slice(c0, c0 + FF_CHUNK)
            hh = _dot(x, w1s_ref[:, cs]) + b1_ref[0, :, cs]
            glu = jnp.minimum(hh[:, :LANES], SWIGLU_LIMIT)
            lin = jnp.clip(hh[:, LANES:], -SWIGLU_LIMIT, SWIGLU_LIMIT)
            acts.append((glu * _sigmoid(SWIGLU_ALPHA * glu) * (lin + 1.0)).astype(BF16))
        act = jnp.concatenate(acts, axis=1)
        y_ref[...] = (_dot(act, w2s_ref[...]) + b2_ref[0]).astype(BF16)

    @pl.when(jnp.logical_not(valid))
    def _():
        y_ref[...] = jnp.zeros_like(y_ref)


def _moe(tile_e, n_valid, xrows, w1, b1, w2, b2, tm):
    n_tiles = xrows.shape[0] // tm
    wsel = lambda i, te, nv: (te[i], 0, 0)
    grid_spec = pltpu.PrefetchScalarGridSpec(
        num_scalar_prefetch=2,
        grid=(n_tiles,),
        in_specs=[pl.BlockSpec((tm, D_MODEL), lambda i, te, nv: (i, 0)),
                  pl.BlockSpec((1, D_MODEL, 2 * D_FF), wsel), pl.BlockSpec((1, 1, 2 * D_FF), wsel),
                  pl.BlockSpec((1, D_FF, D_MODEL), wsel), pl.BlockSpec((1, 1, D_MODEL), wsel)],
        out_specs=pl.BlockSpec((tm, D_MODEL), lambda i, te, nv: (i, 0)),
        scratch_shapes=[pltpu.VMEM((D_MODEL, 2 * D_FF), BF16), pltpu.VMEM((D_FF, D_MODEL), BF16)],
    )
    return pl.pallas_call(
        _moe_kernel,
        grid_spec=grid_spec,
        out_shape=jax.ShapeDtypeStruct(xrows.shape, BF16),
        compiler_params=pltpu.CompilerParams(dimension_semantics=("arbitrary",),
                                             vmem_limit_bytes=VMEM_LIMIT),
        name="moe_experts",
    )(tile_e, n_valid, xrows, w1, b1, w2, b2)


def _final_kernel(h_ref, y0_ref, y1_ref, y2_ref, y3_ref, tw_ref, g_ref, o_ref):
    h = h_ref[...]
    tw = tw_ref[...]
    for k, y_ref in enumerate((y0_ref, y1_ref, y2_ref, y3_ref)):
        h = h + tw[:, k:k + 1] * y_ref[...].astype(F32)
    ms = jnp.mean(h * h, axis=-1, keepdims=True)
    o_ref[...] = h * lax.rsqrt(ms + EPS) * g_ref[...]


def _final(h, yg, tw, g, tm):
    rows = h.shape[0]
    row = lambda i: (i, 0)
    return pl.pallas_call(
        _final_kernel,
        grid=(rows // tm,),
        in_specs=[pl.BlockSpec((tm, D_MODEL), row)] * (1 + TOP_K)
        + [pl.BlockSpec((tm, LANES), row), _const_spec((1, D_MODEL))],
        out_specs=pl.BlockSpec((tm, D_MODEL), row),
        out_shape=jax.ShapeDtypeStruct((rows, D_MODEL), F32),
        compiler_params=pltpu.CompilerParams(dimension_semantics=("parallel",),
                                             vmem_limit_bytes=VMEM_LIMIT),
        name="combine_final_norm",
    )(h, *yg, tw, g)


def _t5_bucket(dist):
    n = jnp.maximum(dist, 0)
    max_exact = N_BUCKETS // 2
    nf = jnp.maximum(n, 1).astype(F32)
    large = max_exact + (jnp.log(nf / max_exact) / math.log(MAX_DISTANCE / max_exact)
                         * (N_BUCKETS - max_exact)).astype(jnp.int32)
    large = jnp.minimum(large, N_BUCKETS - 1)
    return jnp.where(n < max_exact, n, large)


def _bias_tables(rel_bias, nb):
    table = rel_bias.astype(F32)
    j = jnp.arange(BLK, dtype=jnp.int32)[:, None]
    s = jnp.arange(BLK, dtype=jnp.int32)[None, :]
    dist = jnp.concatenate([j + BLK - s, j - s], axis=1)
    valid = (dist >= 0) & (dist < WINDOW)
    def lookup(bucket):
        onehot = (bucket[..., None] == jnp.arange(N_BUCKETS, dtype=jnp.int32)).astype(F32)
        return jnp.einsum('...b,bh->...h', onehot, table, precision=lax.Precision.HIGHEST)

    main = jnp.where(valid[:, :, None], lookup(_t5_bucket(dist)), NEG)
    n_idx = jnp.arange(nb, dtype=jnp.int32)[:, None, None]
    m = jnp.arange(N_META, dtype=jnp.int32)[None, None, :]
    dist_m = N_META + n_idx * BLK + j[None] - m
    meta = lookup(_t5_bucket(dist_m))
    meta = jnp.pad(meta, ((0, 0), (0, 0), (0, LANES - N_META), (0, 0)), constant_values=NEG)
    return main.transpose(2, 0, 1), meta.transpose(0, 3, 1, 2)


def _dup_heads(w):
    w = w.reshape(w.shape[0], N_KV_HEADS, 1, HEAD_DIM)
    return jnp.broadcast_to(w, (w.shape[0], N_KV_HEADS, 2, HEAD_DIM)).reshape(w.shape[0], KV_DUP)


def _pad_lanes(v, value=0.0):
    v = v.reshape(1, -1).astype(F32)
    return jnp.pad(v, ((0, 0), (0, LANES - v.shape[1])), constant_values=value)


def _head_rows(v):
    return jnp.broadcast_to(v.astype(F32)[:, None], (N_SSM_HEADS, BLK))


def _row_tile(rows, target):
    tm = min(rows, target)
    assert rows % tm == 0
    return tm


def kernel(x, meta_tokens, rel_bias, norm_mix_g, w_in, attn_sinks, conv_w, conv_b, dt_bias, a_log,
           d_skip, ssm_norm_g, w_attn_br, w_ssm_br, w_out, norm_ffn_g, w_router, b_router,
           w_exp_in, b_exp_in, w_exp_out, b_exp_out, norm_final_g):
    bsz, seq, _ = x.shape
    assert seq % BLK == 0 and w_in.shape[0] == 1
    nb = seq // BLK
    rows = bsz * seq
    x2d = x.reshape(rows, D_MODEL)

    wi = w_in[0]
    o_k, o_v, o_z = Q_DIM, Q_DIM + KV_DIM, Q_DIM + 2 * KV_DIM
    o_x, o_dt = o_z + D_INNER, o_z + D_INNER + CONV_DIM
    o_g = o_dt + N_SSM_HEADS
    w_proj = jnp.concatenate([
        wi[:, :o_k] * (HEAD_DIM ** -0.5), _dup_heads(wi[:, o_k:o_v]), _dup_heads(wi[:, o_v:o_z]),
        wi[:, o_z:o_x], wi[:, o_x:o_dt], wi[:, o_g:]], axis=1).astype(BF16)
    w_dt = jnp.pad(wi[:, o_dt:o_g], ((0, 0), (0, LANES - N_SSM_HEADS))).astype(BF16)
    g_mix = norm_mix_g[0].reshape(1, D_MODEL)

    tm = _row_tile(rows, 512)
    q, k, v, z, xbc, gate, dtraw = _inproj(x2d, g_mix, w_proj, w_dt, tm)
    _, k_m, v_m, z_m, xbc_m, _, dtraw_m = _inproj(meta_tokens.astype(F32), g_mix, w_proj, w_dt,
                                                  N_META)

    pad_meta = ((BLK - N_META, 0), (0, 0))
    pad_meta_tail = ((0, BLK - N_META), (0, 0))
    bias_main, bias_meta = _bias_tables(rel_bias, nb)
    y_attn = _attention(q, k, v, jnp.pad(k_m, pad_meta_tail), jnp.pad(v_m, pad_meta_tail),
                        bias_main, bias_meta, attn_sinks[0].astype(F32), bsz, nb)

    ssd_params = (conv_w[0].astype(F32), conv_b[0].reshape(1, CONV_DIM).astype(F32),
                  _head_rows(dt_bias[0]), _head_rows(a_log[0]),
                  jnp.repeat(d_skip[0].astype(F32), SSM_HEAD_DIM).reshape(1, D_INNER),
                  ssm_norm_g[0].reshape(1, D_INNER).astype(F32))
    xbc_mp = jnp.pad(xbc_m, pad_meta)
    zero_state = jnp.zeros((N_SSM_PAIRS, D_STATE, LANES), F32)
    _, h_meta = _ssd(xbc_mp, jnp.pad(z_m, pad_meta), jnp.pad(dtraw_m, pad_meta),
                     jnp.zeros((TAIL, CONV_DIM), BF16), zero_state, *ssd_params, 1, 1,
                     BLK - N_META)
    y_ssm, _ = _ssd(xbc, z, dtraw, xbc_mp[BLK - TAIL:], h_meta[0], *ssd_params, bsz, nb, 0)

    w_r = jnp.pad(w_router[0], ((0, 0), (0, LANES - N_EXPERTS))).astype(BF16)
    b_r = _pad_lanes(b_router[0], NEG)
    h1, u2, top_idx, top_w = _merge(
        y_attn, y_ssm, gate, x2d, w_attn_br[0].astype(BF16), w_ssm_br[0].astype(BF16),
        w_out[0].astype(BF16), norm_ffn_g[0].reshape(1, D_MODEL), w_r, b_r, tm)

    tm_e = 512
    n_tiles = -(-(rows * TOP_K + N_EXPERTS * (tm_e - 1)) // tm_e)
    dest_t, info = _route(top_idx, tm_e, tm)
    counts, pstart, pend = info[0, :N_EXPERTS], info[1, :N_EXPERTS], info[2, :N_EXPERTS]
    tile_start = jnp.arange(n_tiles, dtype=jnp.int32) * tm_e
    tile_e = jnp.minimum(jnp.sum(tile_start[:, None] >= pend[None, :], axis=1),
                         N_EXPERTS - 1).astype(jnp.int32)
    n_valid = pend[-1:] // tm_e
    n_assign = rows * TOP_K
    tok_by_row = jnp.argsort(dest_t[:TOP_K].reshape(-1)).astype(jnp.int32) % rows
    tok_by_row = jnp.concatenate([tok_by_row, jnp.zeros((tm_e,), jnp.int32)])
    unpadded_start = (jnp.cumsum(counts) - counts).astype(jnp.int32)
    run_start = jnp.clip(tile_start - (pstart - unpadded_start)[tile_e], 0, n_assign)
    row_tok = jax.vmap(lambda s0: lax.dynamic_slice(tok_by_row, (s0,), (tm_e,)))(
        run_start).reshape(-1)

    b1 = b_exp_in[0].astype(F32).reshape(N_EXPERTS, 2 * D_FF // FF_CHUNK, LANES, 2)
    b1 = b1.transpose(0, 1, 3, 2).reshape(N_EXPERTS, 1, 2 * D_FF)
    y_rows = _moe(tile_e, n_valid, u2[row_tok], w_exp_in[0], b1, w_exp_out[0],
                  b_exp_out[0].reshape(N_EXPERTS, 1, D_MODEL).astype(F32), tm_e)

    yg = [y_rows[dest_t[k]] for k in range(TOP_K)]
    out = _final(h1, yg, top_w, norm_final_g.reshape(1, D_MODEL).astype(F32), tm)
    return out.reshape(bsz, seq, D_MODEL)
```

```python
import functools
import math

import jax
import jax.numpy as jnp
from jax import lax
from jax.experimental import pallas as pl
from jax.experimental.pallas import tpu as pltpu

F32 = jnp.float32
BF16 = jnp.bfloat16

D_MODEL = 1024
N_META = 16
N_HEADS = 16
N_KV_HEADS = 4
HEAD_DIM = 64
Q_PER_KV = N_HEADS // N_KV_HEADS
Q_DIM = N_HEADS * HEAD_DIM
KV_DIM = N_KV_HEADS * HEAD_DIM
WINDOW = 128
BLK = 128
N_BUCKETS = 32
MAX_DISTANCE = 128
D_INNER = 2 * D_MODEL
SSM_HEAD_DIM = 64
N_SSM_HEADS = D_INNER // SSM_HEAD_DIM
N_GROUPS = 4
D_STATE = 128
CONV_WIDTH = 4
BC_DIM = N_GROUPS * D_STATE
CONV_DIM = D_INNER + 2 * BC_DIM
N_EXPERTS = 32
TOP_K = 4
D_FF = D_MODEL
SWIGLU_ALPHA = 1.702
SWIGLU_LIMIT = 7.0
EPS = 1e-5
NEG = -1e30
LOG2E = math.log2(math.e)

LANES = 128
KV_DUP = 2 * KV_DIM
N_PAIRS = N_HEADS // 2
N_SSM_PAIRS = N_SSM_HEADS // 2
CTX = 3 * BLK
VMEM_LIMIT = 56 * 1024 * 1024

PROJ_SEGS = (("q", Q_DIM), ("k", KV_DUP), ("v", KV_DUP), ("z", D_INNER), ("xbc", CONV_DIM),
             ("gate", 2 * D_MODEL))
PROJ_N = sum(w for _, w in PROJ_SEGS)
PROJ_CHUNK = 512


def _dot(a, b):
    return jnp.dot(a, b, preferred_element_type=F32)


def _dot_nt(a, b):
    return lax.dot_general(a, b, (((1,), (1,)), ((), ())), preferred_element_type=F32)


def _sigmoid(x):
    return 1.0 / (1.0 + jnp.exp(-x))


def _const_spec(shape):
    nd = len(shape)
    return pl.BlockSpec(shape, lambda *_: (0,) * nd, pipeline_mode=pl.Buffered(1))


def _inproj_kernel(x_ref, g_ref, w_ref, wdt_ref, q_ref, k_ref, v_ref, z_ref, xbc_ref, gate_ref,
                   dt_ref):
    x = x_ref[...]
    ms = jnp.mean(x * x, axis=-1, keepdims=True)
    u = (x * lax.rsqrt(ms + EPS) * g_ref[...]).astype(BF16)
    outs = (q_ref, k_ref, v_ref, z_ref, xbc_ref, gate_ref)
    off = 0
    for ref, (_, width) in zip(outs, PROJ_SEGS):
        for c0 in range(0, width, PROJ_CHUNK):
            ref[:, c0:c0 + PROJ_CHUNK] = _dot(
                u, w_ref[:, off + c0:off + c0 + PROJ_CHUNK]).astype(BF16)
        off += width
    dt_ref[...] = _dot(u, wdt_ref[...])


def _inproj(x2d, g, w, wdt, tm):
    rows = x2d.shape[0]
    row = lambda i: (i, 0)
    out_shape = [jax.ShapeDtypeStruct((rows, wd), BF16) for _, wd in PROJ_SEGS]
    out_shape.append(jax.ShapeDtypeStruct((rows, LANES), F32))
    out_specs = [pl.BlockSpec((tm, wd), row) for _, wd in PROJ_SEGS]
    out_specs.append(pl.BlockSpec((tm, LANES), row))
    return pl.pallas_call(
        _inproj_kernel,
        grid=(rows // tm,),
        in_specs=[pl.BlockSpec((tm, D_MODEL), row), _const_spec((1, D_MODEL)),
                  _const_spec((D_MODEL, PROJ_N)), _const_spec((D_MODEL, LANES))],
        out_specs=out_specs,
        out_shape=out_shape,
        compiler_params=pltpu.CompilerParams(dimension_semantics=("parallel",),
                                             vmem_limit_bytes=VMEM_LIMIT),
        name="inproj",
    )(x2d, g, w, wdt)


def _attn_kernel(sink_ref, q_ref, kp_ref, kc_ref, km_ref, vp_ref, vc_ref, vm_ref, bwin_ref,
                 bmeta_ref, o_ref, s_ref, p_ref, inv_ref):
    lo = lax.broadcasted_iota(jnp.int32, (1, LANES), 1) < HEAD_DIM
    from_prev = (lax.broadcasted_iota(jnp.int32, (BLK, BLK), 1)
                 > lax.broadcasted_iota(jnp.int32, (BLK, BLK), 0))
    zero = jnp.zeros((), BF16)
    pairs_per_kv = Q_PER_KV // 2

    for g in range(N_KV_HEADS):
        cs = slice(g * LANES, (g + 1) * LANES)
        kk = jnp.concatenate([kp_ref[:, cs], kc_ref[:, cs], km_ref[:, cs]], axis=0)
        k_lo = jnp.where(lo, kk, zero)
        k_hi = jnp.where(lo, zero, kk)
        for j in range(pairs_per_kv):
            pi = g * pairs_per_kv + j
            qp = q_ref[:, pi * LANES:(pi + 1) * LANES]
            s_ref[2 * pi] = _dot_nt(qp, k_lo)
            s_ref[2 * pi + 1] = _dot_nt(qp, k_hi)

    for pi in range(N_PAIRS):
        invs = []
        for t in range(2):
            h = 2 * pi + t
            s_win = (jnp.where(from_prev, s_ref[h, :, :BLK], s_ref[h, :, BLK:2 * BLK])
                     + bwin_ref[0, h])
            s_meta = s_ref[h, :, 2 * BLK:] + bmeta_ref[0, h]
            sink = sink_ref[h]
            m = jnp.maximum(jnp.max(jnp.maximum(s_win, s_meta), axis=-1, keepdims=True), sink)
            p_win = jnp.exp2(s_win - m)
            p_meta = jnp.exp2(s_meta - m)
            den = jnp.sum(p_win + p_meta, axis=-1, keepdims=True) + jnp.exp2(sink - m)
            c0 = t * CTX
            p_ref[pi, :, c0:c0 + BLK] = jnp.where(from_prev, p_win, 0.0).astype(BF16)
            p_ref[pi, :, c0 + BLK:c0 + 2 * BLK] = jnp.where(from_prev, 0.0, p_win).astype(BF16)
            p_ref[pi, :, c0 + 2 * BLK:c0 + CTX] = p_meta.astype(BF16)
            invs.append(1.0 / den)
        inv_ref[pi] = jnp.where(lo, invs[0], invs[1])

    for g in range(N_KV_HEADS):
        cs = slice(g * LANES, (g + 1) * LANES)
        vv = jnp.concatenate([vp_ref[:, cs], vc_ref[:, cs], vm_ref[:, cs]], axis=0)
        v_bd = jnp.concatenate([jnp.where(lo, vv, zero), jnp.where(lo, zero, vv)], axis=0)
        for j in range(pairs_per_kv):
            pi = g * pairs_per_kv + j
            o = _dot(p_ref[pi], v_bd) * inv_ref[pi]
            o_ref[:, pi * LANES:(pi + 1) * LANES] = o.astype(BF16)


def _attention(q, k, v, k_meta, v_meta, bias_win, bias_meta, sinks, bsz, nb):
    cur = lambda b, n, *_: (b * nb + n, 0)
    prev = lambda b, n, *_: (jnp.maximum(b * nb + n - 1, 0), 0)
    const2 = lambda b, n, *_: (0, 0)
    grid_spec = pltpu.PrefetchScalarGridSpec(
        num_scalar_prefetch=1,
        grid=(bsz, nb),
        in_specs=[
            pl.BlockSpec((BLK, Q_DIM), cur),
            pl.BlockSpec((BLK, KV_DUP), prev), pl.BlockSpec((BLK, KV_DUP), cur),
            pl.BlockSpec((BLK, KV_DUP), const2),
            pl.BlockSpec((BLK, KV_DUP), prev), pl.BlockSpec((BLK, KV_DUP), cur),
            pl.BlockSpec((BLK, KV_DUP), const2),
            pl.BlockSpec((1, N_HEADS, BLK, BLK), lambda b, n, *_: (jnp.minimum(n, 1), 0, 0, 0)),
            pl.BlockSpec((1, N_HEADS, BLK, LANES), lambda b, n, *_: (n, 0, 0, 0)),
        ],
        out_specs=pl.BlockSpec((BLK, Q_DIM), cur),
        scratch_shapes=[pltpu.VMEM((N_HEADS, BLK, CTX), F32),
                        pltpu.VMEM((N_PAIRS, BLK, 2 * CTX), BF16),
                        pltpu.VMEM((N_PAIRS, BLK, LANES), F32)],
    )
    return pl.pallas_call(
        _attn_kernel,
        grid_spec=grid_spec,
        out_shape=jax.ShapeDtypeStruct((bsz * nb * BLK, Q_DIM), BF16),
        compiler_params=pltpu.CompilerParams(dimension_semantics=("parallel", "arbitrary"),
                                             vmem_limit_bytes=VMEM_LIMIT),
        name="swa_attention",
    )(sinks, q, k, k, k_meta, v, v, v_meta, bias_win, bias_meta)


CONV_COLS = 256
TAIL = 16


def _ssd_kernel(xbc_ref, z_ref, dtraw_ref, tail0_ref, h0_ref, convw_ref, convb_ref, dtb_ref,
                alog_ref, dskip_ref, ng_ref, y_ref, hfin_ref, win_ref, xc_ref, st_ref, *, n_pad):
    for s in range(xbc_ref.shape[0]):
        _ssd_chunk(xbc_ref.at[s], z_ref.at[s], dtraw_ref.at[s], tail0_ref, h0_ref, convw_ref,
                   convb_ref, dtb_ref, alog_ref, dskip_ref, ng_ref, y_ref.at[s], hfin_ref.at[s],
                   win_ref.at[s], xc_ref.at[s], st_ref.at[s], n_pad)


def _ssd_chunk(xbc_ref, z_ref, dtraw_ref, tail0_ref, h0_ref, convw_ref, convb_ref, dtb_ref,
               alog_ref, dskip_ref, ng_ref, y_ref, hfin_ref, win_ref, xc_ref, st_ref, n_pad):
    c = pl.program_id(1)

    @pl.when(c == 0)
    def _():
        win_ref[0:TAIL, :] = tail0_ref[...]
        st_ref[...] = h0_ref[...]

    win_ref[TAIL:TAIL + BLK, :] = xbc_ref[...]
    sr = lax.broadcasted_iota(jnp.int32, (BLK, TAIL + BLK), 0)
    sc = lax.broadcasted_iota(jnp.int32, (BLK, TAIL + BLK), 1)
    shifts = [jnp.where(sc == sr + (TAIL - (CONV_WIDTH - 1) + w), 1.0, 0.0).astype(BF16)
              for w in range(CONV_WIDTH - 1)]
    live = lax.broadcasted_iota(jnp.int32, (BLK, 1), 0) >= n_pad
    for c0 in range(0, CONV_DIM, CONV_COLS):
        cs = slice(c0, c0 + CONV_COLS)
        wv = win_ref[:, cs]
        acc = convb_ref[:, cs] + convw_ref[CONV_WIDTH - 1:CONV_WIDTH, cs] * wv[TAIL:].astype(F32)
        for w in range(CONV_WIDTH - 1):
            acc = acc + convw_ref[w:w + 1, cs] * _dot(shifts[w], wv)
        act = acc * _sigmoid(acc)
        if n_pad:
            act = jnp.where(live, act, 0.0)
        xc_ref[:, cs] = act
    win_ref[0:TAIL, :] = win_ref[BLK:BLK + TAIL, :]

    lane_t = lax.broadcasted_iota(jnp.int32, (N_SSM_HEADS, BLK), 1)
    x_t = dtraw_ref[...].T[:N_SSM_HEADS] + dtb_ref[...]
    dt_t = jnp.maximum(x_t, 0.0) + jnp.log1p(jnp.exp(-jnp.abs(x_t)))
    if n_pad:
        dt_t = jnp.where(lane_t >= n_pad, dt_t, 0.0)
    a_cs_t = dt_t * (-jnp.exp(alog_ref[...]))
    shift = 1
    while shift < BLK:
        a_cs_t = a_cs_t + jnp.where(lane_t >= shift, pltpu.roll(a_cs_t, shift, 1), 0.0)
        shift *= 2
    w_t = dt_t * jnp.exp(a_cs_t[:, BLK - 1:BLK] - a_cs_t)
    src_t = a_cs_t - jnp.log(dt_t)
    fill = jnp.zeros((BLK - N_SSM_HEADS, BLK), F32)
    a_cs = jnp.concatenate([a_cs_t, fill], axis=0).T
    e_cs = jnp.exp(a_cs)
    ri = lax.broadcasted_iota(jnp.int32, (BLK, BLK), 0)
    ci = lax.broadcasted_iota(jnp.int32, (BLK, BLK), 1)
    causal = ri >= ci
    lo = lax.broadcasted_iota(jnp.int32, (1, LANES), 1) < SSM_HEAD_DIM

    pairs_per_group = N_SSM_PAIRS // N_GROUPS
    for g in range(N_GROUPS):
        b_g = xc_ref[:, D_INNER + g * D_STATE:D_INNER + (g + 1) * D_STATE]
        c_g = xc_ref[:, D_INNER + BC_DIM + g * D_STATE:D_INNER + BC_DIM + (g + 1) * D_STATE]
        cb = _dot_nt(c_g.astype(BF16), b_g.astype(BF16))
        b_gt = b_g.T
        ys = []
        for j in range(pairs_per_group):
            pi = g * pairs_per_group + j
            ps = slice(pi * LANES, (pi + 1) * LANES)
            xs = xc_ref[:, ps]
            xs_b = xs.astype(BF16)
            st = st_ref[pi]
            rhs = jnp.concatenate([xs_b, st.astype(BF16)], axis=0)
            y_h, st_h = [], []
            for t in range(2):
                h = 2 * pi + t
                diff = a_cs[:, h:h + 1] - src_t[h:h + 1, :]
                m_h = cb * jnp.exp(jnp.where(causal, diff, -jnp.inf))
                ce = c_g * e_cs[:, h:h + 1]
                lhs = jnp.concatenate([m_h, ce], axis=1).astype(BF16)
                y_h.append(_dot(lhs, rhs))
                s_new = _dot((b_gt * w_t[h:h + 1, :]).astype(BF16), xs_b)
                st_h.append(st * e_cs[BLK - 1:BLK, h:h + 1] + s_new)
            st_ref[pi] = jnp.where(lo, st_h[0], st_h[1])
            ys.append(jnp.where(lo, y_h[0], y_h[1]) + dskip_ref[:, ps] * xs)
        gs = slice(g * (D_INNER // N_GROUPS), (g + 1) * (D_INNER // N_GROUPS))
        zg = z_ref[:, gs].astype(F32)
        yg = jnp.concatenate(ys, axis=1) * (zg * _sigmoid(zg))
        ms = jnp.mean(yg * yg, axis=-1, keepdims=True)
        y_ref[:, gs] = (yg * lax.rsqrt(ms + EPS) * ng_ref[:, gs]).astype(BF16)

    @pl.when(c == pl.num_programs(1) - 1)
    def _():
        hfin_ref[0] = st_ref[...]


def _ssd(xbc, z, dtraw, tail0, h0, convw, convb, dtb, alog, dskip, ng, bsz, nc, n_pad):
    n_seq = 2 if bsz % 2 == 0 else 1
    nbg = bsz // n_seq
    rows_g = nbg * nc * BLK
    cur = lambda b, c: (0, b * nc + c, 0)
    split = lambda a: a.reshape(n_seq, rows_g, a.shape[-1])
    y, h_fin = pl.pallas_call(
        functools.partial(_ssd_kernel, n_pad=n_pad),
        grid=(nbg, nc),
        in_specs=[
            pl.BlockSpec((n_seq, BLK, CONV_DIM), cur), pl.BlockSpec((n_seq, BLK, D_INNER), cur),
            pl.BlockSpec((n_seq, BLK, LANES), cur),
            _const_spec((TAIL, CONV_DIM)), _const_spec((N_SSM_PAIRS, D_STATE, LANES)),
            _const_spec((CONV_WIDTH, CONV_DIM)), _const_spec((1, CONV_DIM)),
            _const_spec((N_SSM_HEADS, BLK)), _const_spec((N_SSM_HEADS, BLK)),
            _const_spec((1, D_INNER)), _const_spec((1, D_INNER)),
        ],
        out_specs=[pl.BlockSpec((n_seq, BLK, D_INNER), cur),
                   pl.BlockSpec((n_seq, 1, N_SSM_PAIRS, D_STATE, LANES),
                                lambda b, c: (0, b, 0, 0, 0))],
        out_shape=[jax.ShapeDtypeStruct((n_seq, rows_g, D_INNER), BF16),
                   jax.ShapeDtypeStruct((n_seq, nbg, N_SSM_PAIRS, D_STATE, LANES), F32)],
        scratch_shapes=[pltpu.VMEM((n_seq, TAIL + BLK, CONV_DIM), BF16),
                        pltpu.VMEM((n_seq, BLK, CONV_DIM), F32),
                        pltpu.VMEM((n_seq, N_SSM_PAIRS, D_STATE, LANES), F32)],
        compiler_params=pltpu.CompilerParams(dimension_semantics=("parallel", "arbitrary"),
                                             vmem_limit_bytes=VMEM_LIMIT),
        name="ssd_mixer",
    )(split(xbc), split(z), split(dtraw), tail0, h0, convw, convb, dtb, alog, dskip, ng)
    return (y.reshape(bsz * nc * BLK, D_INNER),
            h_fin.reshape(bsz, N_SSM_PAIRS, D_STATE, LANES))


def _merge_kernel(ya_ref, ys_ref, gl_ref, x_ref, wa_ref, ws_ref, wo_ref, g_ref, wr_ref, br_ref,
                  h_ref, u_ref, idx_ref, tw_ref):
    g0 = _sigmoid(gl_ref[:, :D_MODEL].astype(F32))
    g1 = _sigmoid(gl_ref[:, D_MODEL:].astype(F32))
    merged = g0 * _dot(ya_ref[...], wa_ref[...]) + g1 * _dot(ys_ref[...], ws_ref[...])
    h = x_ref[...] + _dot(merged.astype(BF16), wo_ref[...])
    h_ref[...] = h
    ms = jnp.mean(h * h, axis=-1, keepdims=True)
    u = (h * lax.rsqrt(ms + EPS) * g_ref[...]).astype(BF16)
    u_ref[...] = u
    logits = _dot(u, wr_ref[...]) + br_ref[...]
    lane = lax.broadcasted_iota(jnp.int32, logits.shape, 1).astype(F32)
    idx_out = jnp.zeros(logits.shape, F32)
    val_out = jnp.zeros(logits.shape, F32)
    top = None
    den = None
    for k in range(TOP_K):
        m = jnp.max(logits, axis=-1, keepdims=True)
        sel = jnp.min(jnp.where(logits == m, lane, float(LANES)), axis=-1, keepdims=True)
        if k == 0:
            top = m
        e = jnp.exp(m - top)
        den = e if k == 0 else den + e
        idx_out = jnp.where(lane == k, sel, idx_out)
        val_out = jnp.where(lane == k, e, val_out)
        logits = jnp.where(lane == sel, -jnp.inf, logits)
    idx_ref[...] = idx_out.astype(jnp.int32)
    tw_ref[...] = val_out / den


def _merge(ya, ys, gl, x2d, wa, ws, wo, g, wr, br, tm):
    rows = x2d.shape[0]
    row = lambda i: (i, 0)
    return pl.pallas_call(
        _merge_kernel,
        grid=(rows // tm,),
        in_specs=[pl.BlockSpec((tm, Q_DIM), row), pl.BlockSpec((tm, D_INNER), row),
                  pl.BlockSpec((tm, 2 * D_MODEL), row), pl.BlockSpec((tm, D_MODEL), row),
                  _const_spec((Q_DIM, D_MODEL)), _const_spec((D_INNER, D_MODEL)),
                  _const_spec((D_MODEL, D_MODEL)), _const_spec((1, D_MODEL)),
                  _const_spec((D_MODEL, LANES)), _const_spec((1, LANES))],
        out_specs=[pl.BlockSpec((tm, D_MODEL), row), pl.BlockSpec((tm, D_MODEL), row),
                   pl.BlockSpec((tm, LANES), row), pl.BlockSpec((tm, LANES), row)],
        out_shape=[jax.ShapeDtypeStruct((rows, D_MODEL), F32),
                   jax.ShapeDtypeStruct((rows, D_MODEL), BF16),
                   jax.ShapeDtypeStruct((rows, LANES), jnp.int32),
                   jax.ShapeDtypeStruct((rows, LANES), F32)],
        compiler_params=pltpu.CompilerParams(dimension_semantics=("parallel",),
                                             vmem_limit_bytes=VMEM_LIMIT),
        name="merge_router",
    )(ya, ys, gl, x2d, wa, ws, wo, g, wr, br)


def _route_kernel(idx_ref, dest_ref, info_ref, cnt_ref, base_ref, *, tm_e):
    p = pl.program_id(0)
    i = pl.program_id(1)
    tr = idx_ref.shape[0]
    lane = lax.broadcasted_iota(jnp.int32, (tr, LANES), 1)
    idx = idx_ref[...]
    onehot = [jnp.where(lane == idx[:, k:k + 1], 1.0, 0.0) for k in range(TOP_K)]
    oh_all = onehot[0] + onehot[1] + onehot[2] + onehot[3]
    tile_cnt = jnp.sum(oh_all, axis=0, keepdims=True)

    @pl.when(jnp.logical_and(p == 0, i == 0))
    def _():
        cnt_ref[...] = jnp.zeros_like(cnt_ref)

    @pl.when(p == 0)
    def _():
        cnt_ref[...] += tile_cnt

    @pl.when(jnp.logical_and(p == 1, i == 0))
    def _():
        counts = jnp.broadcast_to(cnt_ref[...], (8, LANES))
        padded = jnp.floor((counts + (tm_e - 1)) * (1.0 / tm_e)) * tm_e
        lane8 = lax.broadcasted_iota(jnp.int32, (8, LANES), 1)
        incl = padded
        shift = 1
        while shift < LANES:
            incl = incl + jnp.where(lane8 >= shift, pltpu.roll(incl, shift, 1), 0.0)
            shift *= 2
        pstart = incl - padded
        base_ref[...] = pstart[0:1]
        row8 = lax.broadcasted_iota(jnp.int32, (8, LANES), 0)
        info = jnp.where(row8 == 0, counts, jnp.where(row8 == 1, pstart, incl))
        info_ref[...] = info.astype(jnp.int32)

    @pl.when(p == 1)
    def _():
        r = lax.broadcasted_iota(jnp.int32, (tr, tr), 0)
        c = lax.broadcasted_iota(jnp.int32, (tr, tr), 1)
        before = jnp.where(r > c, 1.0, 0.0).astype(BF16)
        earlier = _dot(before, oh_all.astype(BF16))
        rank = base_ref[...] + earlier
        dest = jnp.zeros((tr, LANES), F32)
        for k in range(TOP_K):
            d_k = jnp.sum(onehot[k] * rank, axis=-1, keepdims=True)
            dest = jnp.where(lane == k, d_k, dest)
        dest_ref[...] = dest.T[0:8].astype(jnp.int32)
        base_ref[...] += tile_cnt


def _route(top_idx, tm_e, tr):
    rows = top_idx.shape[0]
    return pl.pallas_call(
        functools.partial(_route_kernel, tm_e=tm_e),
        grid=(2, rows // tr),
        in_specs=[pl.BlockSpec((tr, LANES), lambda p, i: (i, 0))],
        out_specs=[pl.BlockSpec((8, tr), lambda p, i: (0, i * p)),
                   pl.BlockSpec((8, LANES), lambda p, i: (0, 0))],
        out_shape=[jax.ShapeDtypeStruct((8, rows), jnp.int32),
                   jax.ShapeDtypeStruct((8, LANES), jnp.int32)],
        scratch_shapes=[pltpu.VMEM((1, LANES), F32), pltpu.VMEM((1, LANES), F32)],
        compiler_params=pltpu.CompilerParams(dimension_semantics=("arbitrary", "arbitrary"),
                                             vmem_limit_bytes=VMEM_LIMIT),
        name="route",
    )(top_idx)


FF_CHUNK = 2 * LANES


def _moe_kernel(te_ref, nv_ref, x_ref, w1_ref, b1_ref, w2_ref, b2_ref, y_ref, w1s_ref, w2s_ref):
    i = pl.program_id(0)
    valid = i < nv_ref[0]
    fresh = jnp.logical_or(i == 0, te_ref[i] != te_ref[jnp.maximum(i - 1, 0)])

    @pl.when(jnp.logical_and(valid, fresh))
    def _():
        r = lax.broadcasted_iota(jnp.int32, (FF_CHUNK, FF_CHUNK), 0)
        c = lax.broadcasted_iota(jnp.int32, (FF_CHUNK, FF_CHUNK), 1)
        src = jnp.where(c < LANES, 2 * c, 2 * (c - LANES) + 1)
        perm = jnp.where(r == src, 1.0, 0.0).astype(BF16)
        for c0 in range(0, 2 * D_FF, FF_CHUNK):
            cs = slice(c0, c0 + FF_CHUNK)
            w1s_ref[:, cs] = _dot(w1_ref[0, :, cs].astype(BF16), perm).astype(BF16)
        w2s_ref[...] = w2_ref[0].astype(BF16)

    @pl.when(valid)
    def _():
        x = x_ref[...]
        acts = []
        for c0 in range(0, 2 * D_FF, FF_CHUNK):
            cs = slice(c0, c0 + FF_CHUNK)
            hh = _dot(x, w1s_ref[:, cs]) + b1_ref[0, :, cs]
            glu = jnp.minimum(hh[:, :LANES], SWIGLU_LIMIT)
            lin = jnp.clip(hh[:, LANES:], -SWIGLU_LIMIT, SWIGLU_LIMIT)
            acts.append((glu * _sigmoid(SWIGLU_ALPHA * glu) * (lin + 1.0)).astype(BF16))
        act = jnp.concatenate(acts, axis=1)
        y_ref[...] = (_dot(act, w2s_ref[...]) + b2_ref[0]).astype(BF16)

    @pl.when(jnp.logical_not(valid))
    def _():
        y_ref[...] = jnp.zeros_like(y_ref)


def _moe(tile_e, n_valid, xrows, w1, b1, w2, b2, tm):
    n_tiles = xrows.shape[0] // tm
    wsel = lambda i, te, nv: (te[i], 0, 0)
    grid_spec = pltpu.PrefetchScalarGridSpec(
        num_scalar_prefetch=2,
        grid=(n_tiles,),
        in_specs=[pl.BlockSpec((tm, D_MODEL), lambda i, te, nv: (i, 0)),
                  pl.BlockSpec((1, D_MODEL, 2 * D_FF), wsel), pl.BlockSpec((1, 1, 2 * D_FF), wsel),
                  pl.BlockSpec((1, D_FF, D_MODEL), wsel), pl.BlockSpec((1, 1, D_MODEL), wsel)],
        out_specs=pl.BlockSpec((tm, D_MODEL), lambda i, te, nv: (i, 0)),
        scratch_shapes=[pltpu.VMEM((D_MODEL, 2 * D_FF), BF16), pltpu.VMEM((D_FF, D_MODEL), BF16)],
    )
    return pl.pallas_call(
        _moe_kernel,
        grid_spec=grid_spec,
        out_shape=jax.ShapeDtypeStruct(xrows.shape, BF16),
        compiler_params=pltpu.CompilerParams(dimension_semantics=("arbitrary",),
                                             vmem_limit_bytes=VMEM_LIMIT),
        name="moe_experts",
    )(tile_e, n_valid, xrows, w1, b1, w2, b2)


def _final_kernel(h_ref, y0_ref, y1_ref, y2_ref, y3_ref, tw_ref, g_ref, o_ref):
    h = h_ref[...]
    tw = tw_ref[...]
    for k, y_ref in enumerate((y0_ref, y1_ref, y2_ref, y3_ref)):
        h = h + tw[:, k:k + 1] * y_ref[...].astype(F32)
    ms = jnp.mean(h * h, axis=-1, keepdims=True)
    o_ref[...] = h * lax.rsqrt(ms + EPS) * g_ref[...]


def _final(h, yg, tw, g, tm):
    rows = h.shape[0]
    row = lambda i: (i, 0)
    return pl.pallas_call(
        _final_kernel,
        grid=(rows // tm,),
        in_specs=[pl.BlockSpec((tm, D_MODEL), row)] * (1 + TOP_K)
        + [pl.BlockSpec((tm, LANES), row), _const_spec((1, D_MODEL))],
        out_specs=pl.BlockSpec((tm, D_MODEL), row),
        out_shape=jax.ShapeDtypeStruct((rows, D_MODEL), F32),
        compiler_params=pltpu.CompilerParams(dimension_semantics=("parallel",),
                                             vmem_limit_bytes=VMEM_LIMIT),
        name="combine_final_norm",
    )(h, *yg, tw, g)


def _t5_bucket(dist):
    n = jnp.maximum(dist, 0)
    max_exact = N_BUCKETS // 2
    nf = jnp.maximum(n, 1).astype(F32)
    large = max_exact + (jnp.log(nf / max_exact) / math.log(MAX_DISTANCE / max_exact)
                         * (N_BUCKETS - max_exact)).astype(jnp.int32)
    large = jnp.minimum(large, N_BUCKETS - 1)
    return jnp.where(n < max_exact, n, large)


def _bias_tables(rel_bias, nb):
    table = rel_bias.astype(F32) * LOG2E
    j = jnp.arange(BLK, dtype=jnp.int32)[:, None]
    s = jnp.arange(BLK, dtype=jnp.int32)[None, :]
    from_prev = s > j
    dist = jnp.where(from_prev, j + BLK - s, j - s)

    def lookup(bucket):
        onehot = (bucket[..., None] == jnp.arange(N_BUCKETS, dtype=jnp.int32)).astype(F32)
        return jnp.einsum('...b,bh->...h', onehot, table, precision=lax.Precision.HIGHEST)

    win = lookup(_t5_bucket(dist))
    win = jnp.stack([jnp.where(from_prev[:, :, None], NEG, win), win])
    n_idx = jnp.arange(nb, dtype=jnp.int32)[:, None, None]
    m = jnp.arange(N_META, dtype=jnp.int32)[None, None, :]
    dist_m = N_META + n_idx * BLK + j[None] - m
    meta = lookup(_t5_bucket(dist_m))
    meta = jnp.pad(meta, ((0, 0), (0, 0), (0, LANES - N_META), (0, 0)), constant_values=NEG)
    return win.transpose(0, 3, 1, 2), meta.transpose(0, 3, 1, 2)


def _dup_heads(w):
    w = w.reshape(w.shape[0], N_KV_HEADS, 1, HEAD_DIM)
    return jnp.broadcast_to(w, (w.shape[0], N_KV_HEADS, 2, HEAD_DIM)).reshape(w.shape[0], KV_DUP)


def _pad_lanes(v, value=0.0):
    v = v.reshape(1, -1).astype(F32)
    return jnp.pad(v, ((0, 0), (0, LANES - v.shape[1])), constant_values=value)


def _head_rows(v):
    return jnp.broadcast_to(v.astype(F32)[:, None], (N_SSM_HEADS, BLK))


def _row_tile(rows, target):
    tm = min(rows, target)
    assert rows % tm == 0
    return tm


def kernel(x, meta_tokens, rel_bias, norm_mix_g, w_in, attn_sinks, conv_w, conv_b, dt_bias, a_log,
           d_skip, ssm_norm_g, w_attn_br, w_ssm_br, w_out, norm_ffn_g, w_router, b_router,
           w_exp_in, b_exp_in, w_exp_out, b_exp_out, norm_final_g):
    bsz, seq, _ = x.shape
    assert seq % BLK == 0 and w_in.shape[0] == 1
    nb = seq // BLK
    rows = bsz * seq
    x2d = x.reshape(rows, D_MODEL)

    wi = w_in[0]
    o_k, o_v, o_z = Q_DIM, Q_DIM + KV_DIM, Q_DIM + 2 * KV_DIM
    o_x, o_dt = o_z + D_INNER, o_z + D_INNER + CONV_DIM
    o_g = o_dt + N_SSM_HEADS
    w_proj = jnp.concatenate([
        wi[:, :o_k] * (HEAD_DIM ** -0.5 * LOG2E), _dup_heads(wi[:, o_k:o_v]),
        _dup_heads(wi[:, o_v:o_z]),
        wi[:, o_z:o_x], wi[:, o_x:o_dt], wi[:, o_g:]], axis=1).astype(BF16)
    w_dt = jnp.pad(wi[:, o_dt:o_g], ((0, 0), (0, LANES - N_SSM_HEADS))).astype(BF16)
    g_mix = norm_mix_g[0].reshape(1, D_MODEL)

    tm = _row_tile(rows, 512)
    q, k, v, z, xbc, gate, dtraw = _inproj(x2d, g_mix, w_proj, w_dt, tm)
    _, k_m, v_m, z_m, xbc_m, _, dtraw_m = _inproj(meta_tokens.astype(F32), g_mix, w_proj, w_dt,
                                                  N_META)

    pad_meta = ((BLK - N_META, 0), (0, 0))
    pad_meta_tail = ((0, BLK - N_META), (0, 0))
    bias_win, bias_meta = _bias_tables(rel_bias, nb)
    y_attn = _attention(q, k, v, jnp.pad(k_m, pad_meta_tail), jnp.pad(v_m, pad_meta_tail),
                        bias_win, bias_meta, attn_sinks[0].astype(F32) * LOG2E, bsz, nb)

    ssd_params = (conv_w[0].astype(F32), conv_b[0].reshape(1, CONV_DIM).astype(F32),
                  _head_rows(dt_bias[0]), _head_rows(a_log[0]),
                  jnp.repeat(d_skip[0].astype(F32), SSM_HEAD_DIM).reshape(1, D_INNER),
                  ssm_norm_g[0].reshape(1, D_INNER).astype(F32))
    xbc_mp = jnp.pad(xbc_m, pad_meta)
    zero_state = jnp.zeros((N_SSM_PAIRS, D_STATE, LANES), F32)
    _, h_meta = _ssd(xbc_mp, jnp.pad(z_m, pad_meta), jnp.pad(dtraw_m, pad_meta),
                     jnp.zeros((TAIL, CONV_DIM), BF16), zero_state, *ssd_params, 1, 1,
                     BLK - N_META)
    y_ssm, _ = _ssd(xbc, z, dtraw, xbc_mp[BLK - TAIL:], h_meta[0], *ssd_params, bsz, nb, 0)

    w_r = jnp.pad(w_router[0], ((0, 0), (0, LANES - N_EXPERTS))).astype(BF16)
    b_r = _pad_lanes(b_router[0], NEG)
    h1, u2, top_idx, top_w = _merge(
        y_attn, y_ssm, gate, x2d, w_attn_br[0].astype(BF16), w_ssm_br[0].astype(BF16),
        w_out[0].astype(BF16), norm_ffn_g[0].reshape(1, D_MODEL), w_r, b_r, tm)

    tm_e = 512
    n_tiles = -(-(rows * TOP_K + N_EXPERTS * (tm_e - 1)) // tm_e)
    dest_t, info = _route(top_idx, tm_e, tm)
    counts, pstart, pend = info[0, :N_EXPERTS], info[1, :N_EXPERTS], info[2, :N_EXPERTS]
    tile_start = jnp.arange(n_tiles, dtype=jnp.int32) * tm_e
    tile_e = jnp.minimum(jnp.sum(tile_start[:, None] >= pend[None, :], axis=1),
                         N_EXPERTS - 1).astype(jnp.int32)
    n_valid = pend[-1:] // tm_e
    n_rows = n_tiles * tm_e
    n_assign = rows * TOP_K
    slot = jnp.arange(tm_e, dtype=jnp.int32)[None, :]
    pad_key = (pstart + counts)[:, None] + slot
    pad_key = jnp.where(pad_key < pend[:, None], pad_key, n_rows).reshape(-1)
    tail_key = pend[-1] + jnp.arange(n_rows - n_assign, dtype=jnp.int32)
    tail_key = jnp.where(tail_key < n_rows, tail_key, n_rows)
    n_fill = pad_key.shape[0] + tail_key.shape[0]
    keys = jnp.concatenate([dest_t[:TOP_K].reshape(-1), pad_key, tail_key])
    toks = jnp.concatenate([jnp.tile(jnp.arange(rows, dtype=jnp.int32), TOP_K),
                            jnp.arange(n_fill, dtype=jnp.int32) % rows])
    row_tok = lax.sort((keys, toks), num_keys=1)[1][:n_rows]

    b1 = b_exp_in[0].astype(F32).reshape(N_EXPERTS, 2 * D_FF // FF_CHUNK, LANES, 2)
    b1 = b1.transpose(0, 1, 3, 2).reshape(N_EXPERTS, 1, 2 * D_FF)
    y_rows = _moe(tile_e, n_valid, u2[row_tok], w_exp_in[0], b1, w_exp_out[0],
                  b_exp_out[0].reshape(N_EXPERTS, 1, D_MODEL).astype(F32), tm_e)

    yg = [y_rows[dest_t[k]] for k in range(TOP_K)]
    out = _final(h1, yg, top_w, norm_final_g.reshape(1, D_MODEL).astype(F32), tm)
    return out.reshape(bsz, seq, D_MODEL)
```

```python
import functools
import math

import jax
import jax.numpy as jnp
from jax import lax
from jax.experimental import pallas as pl
from jax.experimental.pallas import tpu as pltpu

F32 = jnp.float32
BF16 = jnp.bfloat16

D_MODEL = 1024
N_META = 16
N_HEADS = 16
N_KV_HEADS = 4
HEAD_DIM = 64
Q_PER_KV = N_HEADS // N_KV_HEADS
Q_DIM = N_HEADS * HEAD_DIM
KV_DIM = N_KV_HEADS * HEAD_DIM
WINDOW = 128
BLK = 128
N_BUCKETS = 32
MAX_DISTANCE = 128
D_INNER = 2 * D_MODEL
SSM_HEAD_DIM = 64
N_SSM_HEADS = D_INNER // SSM_HEAD_DIM
N_GROUPS = 4
D_STATE = 128
CONV_WIDTH = 4
BC_DIM = N_GROUPS * D_STATE
CONV_DIM = D_INNER + 2 * BC_DIM
N_EXPERTS = 32
TOP_K = 4
D_FF = D_MODEL
SWIGLU_ALPHA = 1.702
SWIGLU_LIMIT = 7.0
EPS = 1e-5
NEG = -1e30
LOG2E = math.log2(math.e)

LANES = 128
KV_DUP = 2 * KV_DIM
N_PAIRS = N_HEADS // 2
N_SSM_PAIRS = N_SSM_HEADS // 2
CTX = 3 * BLK
VMEM_LIMIT = 56 * 1024 * 1024

PROJ_SEGS = (("q", Q_DIM), ("k", KV_DUP), ("v", KV_DUP), ("z", D_INNER), ("xbc", CONV_DIM),
             ("gate", 2 * D_MODEL))
PROJ_N = sum(w for _, w in PROJ_SEGS)
PROJ_CHUNK = 512


def _dot(a, b):
    return jnp.dot(a, b, preferred_element_type=F32)


def _dot_nt(a, b):
    return lax.dot_general(a, b, (((1,), (1,)), ((), ())), preferred_element_type=F32)


def _sigmoid(x):
    return 0.5 * jnp.tanh(0.5 * x) + 0.5


def _silu(x):
    h = 0.5 * x
    return h * jnp.tanh(h) + h


def _const_spec(shape):
    nd = len(shape)
    return pl.BlockSpec(shape, lambda *_: (0,) * nd, pipeline_mode=pl.Buffered(1))


def _inproj_kernel(x_ref, g_ref, w_ref, wdt_ref, q_ref, k_ref, v_ref, z_ref, xbc_ref, gate_ref,
                   dt_ref):
    x = x_ref[...]
    ms = jnp.mean(x * x, axis=-1, keepdims=True)
    u = (x * lax.rsqrt(ms + EPS) * g_ref[...]).astype(BF16)
    outs = (q_ref, k_ref, v_ref, z_ref, xbc_ref, gate_ref)
    off = 0
    for ref, (_, width) in zip(outs, PROJ_SEGS):
        for c0 in range(0, width, PROJ_CHUNK):
            ref[:, c0:c0 + PROJ_CHUNK] = _dot(
                u, w_ref[:, off + c0:off + c0 + PROJ_CHUNK]).astype(BF16)
        off += width
    dt_ref[...] = _dot(u, wdt_ref[...])


def _inproj(x2d, g, w, wdt, tm):
    rows = x2d.shape[0]
    row = lambda i: (i, 0)
    out_shape = [jax.ShapeDtypeStruct((rows, wd), BF16) for _, wd in PROJ_SEGS]
    out_shape.append(jax.ShapeDtypeStruct((rows, LANES), F32))
    out_specs = [pl.BlockSpec((tm, wd), row) for _, wd in PROJ_SEGS]
    out_specs.append(pl.BlockSpec((tm, LANES), row))
    return pl.pallas_call(
        _inproj_kernel,
        grid=(rows // tm,),
        in_specs=[pl.BlockSpec((tm, D_MODEL), row), _const_spec((1, D_MODEL)),
                  _const_spec((D_MODEL, PROJ_N)), _const_spec((D_MODEL, LANES))],
        out_specs=out_specs,
        out_shape=out_shape,
        compiler_params=pltpu.CompilerParams(dimension_semantics=("parallel",),
                                             vmem_limit_bytes=VMEM_LIMIT),
        name="inproj",
    )(x2d, g, w, wdt)


def _attn_kernel(sink_ref, q_ref, kp_ref, kc_ref, km_ref, vp_ref, vc_ref, vm_ref, bwin_ref,
                 bmeta_ref, o_ref, s_ref, p_ref, inv_ref):
    lo = lax.broadcasted_iota(jnp.int32, (1, LANES), 1) < HEAD_DIM
    from_prev = (lax.broadcasted_iota(jnp.int32, (BLK, BLK), 1)
                 > lax.broadcasted_iota(jnp.int32, (BLK, BLK), 0))
    zero = jnp.zeros((), BF16)
    pairs_per_kv = Q_PER_KV // 2

    for g in range(N_KV_HEADS):
        cs = slice(g * LANES, (g + 1) * LANES)
        kk = jnp.concatenate([kp_ref[:, cs], kc_ref[:, cs], km_ref[:, cs]], axis=0)
        k_lo = jnp.where(lo, kk, zero)
        k_hi = jnp.where(lo, zero, kk)
        for j in range(pairs_per_kv):
            pi = g * pairs_per_kv + j
            qp = q_ref[:, pi * LANES:(pi + 1) * LANES]
            s_ref[2 * pi] = _dot_nt(qp, k_lo)
            s_ref[2 * pi + 1] = _dot_nt(qp, k_hi)

    for pi in range(N_PAIRS):
        invs = []
        for t in range(2):
            h = 2 * pi + t
            s_win = (jnp.where(from_prev, s_ref[h, :, :BLK], s_ref[h, :, BLK:2 * BLK])
                     + bwin_ref[0, h])
            s_meta = s_ref[h, :, 2 * BLK:] + bmeta_ref[0, h]
            sink = sink_ref[h]
            m = jnp.maximum(jnp.max(jnp.maximum(s_win, s_meta), axis=-1, keepdims=True), sink)
            p_win = jnp.exp2(s_win - m)
            p_meta = jnp.exp2(s_meta - m)
            den = jnp.sum(p_win + p_meta, axis=-1, keepdims=True) + jnp.exp2(sink - m)
            c0 = t * CTX
            p_ref[pi, :, c0:c0 + BLK] = jnp.where(from_prev, p_win, 0.0).astype(BF16)
            p_ref[pi, :, c0 + BLK:c0 + 2 * BLK] = jnp.where(from_prev, 0.0, p_win).astype(BF16)
            p_ref[pi, :, c0 + 2 * BLK:c0 + CTX] = p_meta.astype(BF16)
            invs.append(1.0 / den)
        inv_ref[pi] = jnp.where(lo, invs[0], invs[1])

    for g in range(N_KV_HEADS):
        cs = slice(g * LANES, (g + 1) * LANES)
        vv = jnp.concatenate([vp_ref[:, cs], vc_ref[:, cs], vm_ref[:, cs]], axis=0)
        v_bd = jnp.concatenate([jnp.where(lo, vv, zero), jnp.where(lo, zero, vv)], axis=0)
        for j in range(pairs_per_kv):
            pi = g * pairs_per_kv + j
            o = _dot(p_ref[pi], v_bd) * inv_ref[pi]
            o_ref[:, pi * LANES:(pi + 1) * LANES] = o.astype(BF16)


def _attention(q, k, v, k_meta, v_meta, bias_win, bias_meta, sinks, bsz, nb):
    cur = lambda b, n, *_: (b * nb + n, 0)
    prev = lambda b, n, *_: (jnp.maximum(b * nb + n - 1, 0), 0)
    const2 = lambda b, n, *_: (0, 0)
    grid_spec = pltpu.PrefetchScalarGridSpec(
        num_scalar_prefetch=1,
        grid=(bsz, nb),
        in_specs=[
            pl.BlockSpec((BLK, Q_DIM), cur),
            pl.BlockSpec((BLK, KV_DUP), prev), pl.BlockSpec((BLK, KV_DUP), cur),
            pl.BlockSpec((BLK, KV_DUP), const2),
            pl.BlockSpec((BLK, KV_DUP), prev), pl.BlockSpec((BLK, KV_DUP), cur),
            pl.BlockSpec((BLK, KV_DUP), const2),
            pl.BlockSpec((1, N_HEADS, BLK, BLK), lambda b, n, *_: (jnp.minimum(n, 1), 0, 0, 0)),
            pl.BlockSpec((1, N_HEADS, BLK, LANES), lambda b, n, *_: (n, 0, 0, 0)),
        ],
        out_specs=pl.BlockSpec((BLK, Q_DIM), cur),
        scratch_shapes=[pltpu.VMEM((N_HEADS, BLK, CTX), F32),
                        pltpu.VMEM((N_PAIRS, BLK, 2 * CTX), BF16),
                        pltpu.VMEM((N_PAIRS, BLK, LANES), F32)],
    )
    return pl.pallas_call(
        _attn_kernel,
        grid_spec=grid_spec,
        out_shape=jax.ShapeDtypeStruct((bsz * nb * BLK, Q_DIM), BF16),
        compiler_params=pltpu.CompilerParams(dimension_semantics=("parallel", "arbitrary"),
                                             vmem_limit_bytes=VMEM_LIMIT),
        name="swa_attention",
    )(sinks, q, k, k, k_meta, v, v, v_meta, bias_win, bias_meta)


CONV_COLS = 256
TAIL = 16


def _ssd_kernel(xbc_ref, z_ref, dtraw_ref, tail0_ref, h0_ref, convw_ref, convb_ref, dtb_ref,
                alog_ref, dskip_ref, ng_ref, y_ref, hfin_ref, win_ref, xc_ref, st_ref, *, n_pad):
    for s in range(xbc_ref.shape[0]):
        _ssd_chunk(xbc_ref.at[s], z_ref.at[s], dtraw_ref.at[s], tail0_ref, h0_ref, convw_ref,
                   convb_ref, dtb_ref, alog_ref, dskip_ref, ng_ref, y_ref.at[s], hfin_ref.at[s],
                   win_ref.at[s], xc_ref.at[s], st_ref.at[s], n_pad)


def _ssd_chunk(xbc_ref, z_ref, dtraw_ref, tail0_ref, h0_ref, convw_ref, convb_ref, dtb_ref,
               alog_ref, dskip_ref, ng_ref, y_ref, hfin_ref, win_ref, xc_ref, st_ref, n_pad):
    c = pl.program_id(1)

    @pl.when(c == 0)
    def _():
        win_ref[0:TAIL, :] = tail0_ref[...]
        st_ref[...] = h0_ref[...]

    win_ref[TAIL:TAIL + BLK, :] = xbc_ref[...]
    sr = lax.broadcasted_iota(jnp.int32, (BLK, TAIL + BLK), 0)
    sc = lax.broadcasted_iota(jnp.int32, (BLK, TAIL + BLK), 1)
    shifts = [jnp.where(sc == sr + (TAIL - (CONV_WIDTH - 1) + w), 1.0, 0.0).astype(BF16)
              for w in range(CONV_WIDTH)]
    live = lax.broadcasted_iota(jnp.int32, (BLK, 1), 0) >= n_pad
    for c0 in range(0, CONV_DIM, CONV_COLS):
        cs = slice(c0, c0 + CONV_COLS)
        wv = win_ref[:, cs]
        acc = convb_ref[:, cs]
        for w in range(CONV_WIDTH):
            acc = acc + convw_ref[w:w + 1, cs] * _dot(shifts[w], wv)
        act = _silu(acc)
        if n_pad:
            act = jnp.where(live, act, 0.0)
        xc_ref[:, cs] = act
    win_ref[0:TAIL, :] = win_ref[BLK:BLK + TAIL, :]

    lane_t = lax.broadcasted_iota(jnp.int32, (N_SSM_HEADS, BLK), 1)
    x_t = dtraw_ref[...].T[:N_SSM_HEADS] + dtb_ref[...]
    dt_t = jnp.maximum(x_t, 0.0) + jnp.log1p(jnp.exp(-jnp.abs(x_t)))
    if n_pad:
        dt_t = jnp.where(lane_t >= n_pad, dt_t, 0.0)
    a_cs_t = dt_t * (-jnp.exp(alog_ref[...]))
    shift = 1
    while shift < BLK:
        a_cs_t = a_cs_t + jnp.where(lane_t >= shift, pltpu.roll(a_cs_t, shift, 1), 0.0)
        shift *= 2
    w_t = dt_t * jnp.exp(a_cs_t[:, BLK - 1:BLK] - a_cs_t)
    src_t = a_cs_t - jnp.log(dt_t)
    fill = jnp.zeros((BLK - N_SSM_HEADS, BLK), F32)
    a_cs = jnp.concatenate([a_cs_t, fill], axis=0).T
    e_cs = jnp.exp(a_cs)
    ri = lax.broadcasted_iota(jnp.int32, (BLK, BLK), 0)
    ci = lax.broadcasted_iota(jnp.int32, (BLK, BLK), 1)
    causal = ri >= ci
    lo = lax.broadcasted_iota(jnp.int32, (1, LANES), 1) < SSM_HEAD_DIM

    pairs_per_group = N_SSM_PAIRS // N_GROUPS
    for g in range(N_GROUPS):
        b_g = xc_ref[:, D_INNER + g * D_STATE:D_INNER + (g + 1) * D_STATE]
        c_g = xc_ref[:, D_INNER + BC_DIM + g * D_STATE:D_INNER + BC_DIM + (g + 1) * D_STATE]
        cb = _dot_nt(c_g.astype(BF16), b_g.astype(BF16))
        b_gt = b_g.T
        ys = []
        for j in range(pairs_per_group):
            pi = g * pairs_per_group + j
            ps = slice(pi * LANES, (pi + 1) * LANES)
            xs = xc_ref[:, ps]
            xs_b = xs.astype(BF16)
            st = st_ref[pi]
            rhs = jnp.concatenate([xs_b, st.astype(BF16)], axis=0)
            y_h, st_h = [], []
            for t in range(2):
                h = 2 * pi + t
                diff = a_cs[:, h:h + 1] - src_t[h:h + 1, :]
                m_h = cb * jnp.exp(jnp.where(causal, diff, -jnp.inf))
                ce = c_g * e_cs[:, h:h + 1]
                lhs = jnp.concatenate([m_h, ce], axis=1).astype(BF16)
                y_h.append(_dot(lhs, rhs))
                s_new = _dot((b_gt * w_t[h:h + 1, :]).astype(BF16), xs_b)
                st_h.append(st * e_cs[BLK - 1:BLK, h:h + 1] + s_new)
            st_ref[pi] = jnp.where(lo, st_h[0], st_h[1])
            ys.append(jnp.where(lo, y_h[0], y_h[1]) + dskip_ref[:, ps] * xs)
        gs = slice(g * (D_INNER // N_GROUPS), (g + 1) * (D_INNER // N_GROUPS))
        zg = z_ref[:, gs].astype(F32)
        yg = jnp.concatenate(ys, axis=1) * _silu(zg)
        ms = jnp.mean(yg * yg, axis=-1, keepdims=True)
        y_ref[:, gs] = (yg * lax.rsqrt(ms + EPS) * ng_ref[:, gs]).astype(BF16)

    @pl.when(c == pl.num_programs(1) - 1)
    def _():
        hfin_ref[0] = st_ref[...]


def _ssd(xbc, z, dtraw, tail0, h0, convw, convb, dtb, alog, dskip, ng, bsz, nc, n_pad):
    n_seq = 2 if bsz % 2 == 0 else 1
    nbg = bsz // n_seq
    rows_g = nbg * nc * BLK
    cur = lambda b, c: (0, b * nc + c, 0)
    split = lambda a: a.reshape(n_seq, rows_g, a.shape[-1])
    y, h_fin = pl.pallas_call(
        functools.partial(_ssd_kernel, n_pad=n_pad),
        grid=(nbg, nc),
        in_specs=[
            pl.BlockSpec((n_seq, BLK, CONV_DIM), cur), pl.BlockSpec((n_seq, BLK, D_INNER), cur),
            pl.BlockSpec((n_seq, BLK, LANES), cur),
            _const_spec((TAIL, CONV_DIM)), _const_spec((N_SSM_PAIRS, D_STATE, LANES)),
            _const_spec((CONV_WIDTH, CONV_DIM)), _const_spec((1, CONV_DIM)),
            _const_spec((N_SSM_HEADS, BLK)), _const_spec((N_SSM_HEADS, BLK)),
            _const_spec((1, D_INNER)), _const_spec((1, D_INNER)),
        ],
        out_specs=[pl.BlockSpec((n_seq, BLK, D_INNER), cur),
                   pl.BlockSpec((n_seq, 1, N_SSM_PAIRS, D_STATE, LANES),
                                lambda b, c: (0, b, 0, 0, 0))],
        out_shape=[jax.ShapeDtypeStruct((n_seq, rows_g, D_INNER), BF16),
                   jax.ShapeDtypeStruct((n_seq, nbg, N_SSM_PAIRS, D_STATE, LANES), F32)],
        scratch_shapes=[pltpu.VMEM((n_seq, TAIL + BLK, CONV_DIM), BF16),
                        pltpu.VMEM((n_seq, BLK, CONV_DIM), F32),
                        pltpu.VMEM((n_seq, N_SSM_PAIRS, D_STATE, LANES), F32)],
        compiler_params=pltpu.CompilerParams(dimension_semantics=("parallel", "arbitrary"),
                                             vmem_limit_bytes=VMEM_LIMIT),
        name="ssd_mixer",
    )(split(xbc), split(z), split(dtraw), tail0, h0, convw, convb, dtb, alog, dskip, ng)
    return (y.reshape(bsz * nc * BLK, D_INNER),
            h_fin.reshape(bsz, N_SSM_PAIRS, D_STATE, LANES))


def _merge_kernel(ya_ref, ys_ref, gl_ref, x_ref, wa_ref, ws_ref, wo_ref, g_ref, wr_ref, br_ref,
                  h_ref, u_ref, idx_ref, tw_ref):
    g0 = _sigmoid(gl_ref[:, :D_MODEL].astype(F32))
    g1 = _sigmoid(gl_ref[:, D_MODEL:].astype(F32))
    merged = g0 * _dot(ya_ref[...], wa_ref[...]) + g1 * _dot(ys_ref[...], ws_ref[...])
    h = x_ref[...] + _dot(merged.astype(BF16), wo_ref[...])
    h_ref[...] = h
    ms = jnp.mean(h * h, axis=-1, keepdims=True)
    u = (h * lax.rsqrt(ms + EPS) * g_ref[...]).astype(BF16)
    u_ref[...] = u
    logits = _dot(u, wr_ref[...]) + br_ref[...]
    lane = lax.broadcasted_iota(jnp.int32, logits.shape, 1).astype(F32)
    idx_out = jnp.zeros(logits.shape, F32)
    val_out = jnp.zeros(logits.shape, F32)
    top = None
    den = None
    for k in range(TOP_K):
        m = jnp.max(logits, axis=-1, keepdims=True)
        sel = jnp.min(jnp.where(logits == m, lane, float(LANES)), axis=-1, keepdims=True)
        if k == 0:
            top = m
        e = jnp.exp(m - top)
        den = e if k == 0 else den + e
        idx_out = jnp.where(lane == k, sel, idx_out)
        val_out = jnp.where(lane == k, e, val_out)
        logits = jnp.where(lane == sel, -jnp.inf, logits)
    idx_ref[...] = idx_out.astype(jnp.int32)
    tw_ref[...] = val_out / den


def _merge(ya, ys, gl, x2d, wa, ws, wo, g, wr, br, tm):
    rows = x2d.shape[0]
    row = lambda i: (i, 0)
    return pl.pallas_call(
        _merge_kernel,
        grid=(rows // tm,),
        in_specs=[pl.BlockSpec((tm, Q_DIM), row), pl.BlockSpec((tm, D_INNER), row),
                  pl.BlockSpec((tm, 2 * D_MODEL), row), pl.BlockSpec((tm, D_MODEL), row),
                  _const_spec((Q_DIM, D_MODEL)), _const_spec((D_INNER, D_MODEL)),
                  _const_spec((D_MODEL, D_MODEL)), _const_spec((1, D_MODEL)),
                  _const_spec((D_MODEL, LANES)), _const_spec((1, LANES))],
        out_specs=[pl.BlockSpec((tm, D_MODEL), row), pl.BlockSpec((tm, D_MODEL), row),
                   pl.BlockSpec((tm, LANES), row), pl.BlockSpec((tm, LANES), row)],
        out_shape=[jax.ShapeDtypeStruct((rows, D_MODEL), F32),
                   jax.ShapeDtypeStruct((rows, D_MODEL), BF16),
                   jax.ShapeDtypeStruct((rows, LANES), jnp.int32),
                   jax.ShapeDtypeStruct((rows, LANES), F32)],
        compiler_params=pltpu.CompilerParams(dimension_semantics=("parallel",),
                                             vmem_limit_bytes=VMEM_LIMIT),
        name="merge_router",
    )(ya, ys, gl, x2d, wa, ws, wo, g, wr, br)


def _route_kernel(idx_ref, dest_ref, info_ref, cnt_ref, base_ref, *, tm_e):
    p = pl.program_id(0)
    i = pl.program_id(1)
    tr = idx_ref.shape[0]
    lane = lax.broadcasted_iota(jnp.int32, (tr, LANES), 1)
    idx = idx_ref[...]
    onehot = [jnp.where(lane == idx[:, k:k + 1], 1.0, 0.0) for k in range(TOP_K)]
    oh_all = onehot[0] + onehot[1] + onehot[2] + onehot[3]
    tile_cnt = jnp.sum(oh_all, axis=0, keepdims=True)

    @pl.when(jnp.logical_and(p == 0, i == 0))
    def _():
        cnt_ref[...] = jnp.zeros_like(cnt_ref)

    @pl.when(p == 0)
    def _():
        cnt_ref[...] += tile_cnt

    @pl.when(jnp.logical_and(p == 1, i == 0))
    def _():
        counts = jnp.broadcast_to(cnt_ref[...], (8, LANES))
        padded = jnp.floor((counts + (tm_e - 1)) * (1.0 / tm_e)) * tm_e
        lane8 = lax.broadcasted_iota(jnp.int32, (8, LANES), 1)
        incl = padded
        shift = 1
        while shift < LANES:
            incl = incl + jnp.where(lane8 >= shift, pltpu.roll(incl, shift, 1), 0.0)
            shift *= 2
        pstart = incl - padded
        base_ref[...] = pstart[0:1]
        row8 = lax.broadcasted_iota(jnp.int32, (8, LANES), 0)
        info = jnp.where(row8 == 0, counts, jnp.where(row8 == 1, pstart, incl))
        info_ref[...] = info.astype(jnp.int32)

    @pl.when(p == 1)
    def _():
        r = lax.broadcasted_iota(jnp.int32, (tr, tr), 0)
        c = lax.broadcasted_iota(jnp.int32, (tr, tr), 1)
        before = jnp.where(r > c, 1.0, 0.0).astype(BF16)
        earlier = _dot(before, oh_all.astype(BF16))
        rank = base_ref[...] + earlier
        dest = jnp.zeros((tr, LANES), F32)
        for k in range(TOP_K):
            d_k = jnp.sum(onehot[k] * rank, axis=-1, keepdims=True)
            dest = jnp.where(lane == k, d_k, dest)
        dest_ref[...] = dest.T[0:8].astype(jnp.int32)
        base_ref[...] += tile_cnt


def _route(top_idx, tm_e, tr):
    rows = top_idx.shape[0]
    return pl.pallas_call(
        functools.partial(_route_kernel, tm_e=tm_e),
        grid=(2, rows // tr),
        in_specs=[pl.BlockSpec((tr, LANES), lambda p, i: (i, 0))],
        out_specs=[pl.BlockSpec((8, tr), lambda p, i: (0, i * p)),
                   pl.BlockSpec((8, LANES), lambda p, i: (0, 0))],
        out_shape=[jax.ShapeDtypeStruct((8, rows), jnp.int32),
                   jax.ShapeDtypeStruct((8, LANES), jnp.int32)],
        scratch_shapes=[pltpu.VMEM((1, LANES), F32), pltpu.VMEM((1, LANES), F32)],
        compiler_params=pltpu.CompilerParams(dimension_semantics=("arbitrary", "arbitrary"),
                                             vmem_limit_bytes=VMEM_LIMIT),
        name="route",
    )(top_idx)


FF_CHUNK = 2 * LANES


def _moe_kernel(te_ref, nv_ref, x_ref, w1_ref, b1_ref, w2_ref, b2_ref, y_ref, w1s_ref, w2s_ref):
    i = pl.program_id(0)
    valid = i < nv_ref[0]
    fresh = jnp.logical_or(i == 0, te_ref[i] != te_ref[jnp.maximum(i - 1, 0)])

    @pl.when(jnp.logical_and(valid, fresh))
    def _():
        r = lax.broadcasted_iota(jnp.int32, (FF_CHUNK, FF_CHUNK), 0)
        c = lax.broadcasted_iota(jnp.int32, (FF_CHUNK, FF_CHUNK), 1)
        src = jnp.where(c < LANES, 2 * c, 2 * (c - LANES) + 1)
        perm = jnp.where(r == src, 1.0, 0.0).astype(BF16)
        for c0 in range(0, 2 * D_FF, FF_CHUNK):
            cs = slice(c0, c0 + FF_CHUNK)
            w1s_ref[:, cs] = _dot(w1_ref[0, :, cs].astype(BF16), perm).astype(BF16)
        w2s_ref[...] = w2_ref[0].astype(BF16)

    @pl.when(valid)
    def _():
        x = x_ref[...]
        acts = []
        for c0 in range(0, 2 * D_FF, FF_CHUNK):
            cs = slice(c0, c0 + FF_CHUNK)
            hh = _dot(x, w1s_ref[:, cs]) + b1_ref[0, :, cs]
            glu = jnp.minimum(hh[:, :LANES], SWIGLU_LIMIT)
            lin = jnp.clip(hh[:, LANES:], -SWIGLU_LIMIT, SWIGLU_LIMIT)
            acts.append((glu * _sigmoid(SWIGLU_ALPHA * glu) * (lin + 1.0)).astype(BF16))
        act = jnp.concatenate(acts, axis=1)
        y_ref[...] = (_dot(act, w2s_ref[...]) + b2_ref[0]).astype(BF16)

    @pl.when(jnp.logical_not(valid))
    def _():
        y_ref[...] = jnp.zeros_like(y_ref)


def _moe_kernel_into(te_ref, nv_ref, x_ref, w1_ref, b1_ref, w2_ref, b2_ref, yprev_ref, y_ref,
                     w1s_ref, w2s_ref):
    del yprev_ref
    _moe_kernel(te_ref, nv_ref, x_ref, w1_ref, b1_ref, w2_ref, b2_ref, y_ref, w1s_ref, w2s_ref)


def _moe(tile_e, n_valid, xrows, w1, b1, w2, b2, tm, tile0, n_tiles_total, y_prev=None):
    n_tiles = xrows.shape[0] // tm
    wsel = lambda i, te, nv: (te[i], 0, 0)
    in_specs = [pl.BlockSpec((tm, D_MODEL), lambda i, te, nv: (i, 0)),
                pl.BlockSpec((1, D_MODEL, 2 * D_FF), wsel), pl.BlockSpec((1, 1, 2 * D_FF), wsel),
                pl.BlockSpec((1, D_FF, D_MODEL), wsel), pl.BlockSpec((1, 1, D_MODEL), wsel)]
    args = [tile_e, n_valid, xrows, w1, b1, w2, b2]
    if y_prev is not None:
        in_specs.append(pl.BlockSpec(memory_space=pl.ANY))
        args.append(y_prev)
    grid_spec = pltpu.PrefetchScalarGridSpec(
        num_scalar_prefetch=2,
        grid=(n_tiles,),
        in_specs=in_specs,
        out_specs=pl.BlockSpec((tm, D_MODEL), lambda i, te, nv: (i + tile0, 0)),
        scratch_shapes=[pltpu.VMEM((D_MODEL, 2 * D_FF), BF16), pltpu.VMEM((D_FF, D_MODEL), BF16)],
    )
    return pl.pallas_call(
        _moe_kernel if y_prev is None else _moe_kernel_into,
        grid_spec=grid_spec,
        out_shape=jax.ShapeDtypeStruct((n_tiles_total * tm, D_MODEL), BF16),
        input_output_aliases={} if y_prev is None else {len(args) - 1: 0},
        compiler_params=pltpu.CompilerParams(dimension_semantics=("arbitrary",),
                                             vmem_limit_bytes=VMEM_LIMIT),
        name="moe_experts",
    )(*args)


def _final_kernel(h_ref, y0_ref, y1_ref, y2_ref, y3_ref, tw_ref, g_ref, o_ref):
    h = h_ref[...]
    tw = tw_ref[...]
    for k, y_ref in enumerate((y0_ref, y1_ref, y2_ref, y3_ref)):
        h = h + tw[:, k:k + 1] * y_ref[...].astype(F32)
    ms = jnp.mean(h * h, axis=-1, keepdims=True)
    o_ref[...] = h * lax.rsqrt(ms + EPS) * g_ref[...]


def _final(h, yg, tw, g, tm):
    rows = h.shape[0]
    row = lambda i: (i, 0)
    return pl.pallas_call(
        _final_kernel,
        grid=(rows // tm,),
        in_specs=[pl.BlockSpec((tm, D_MODEL), row)] * (1 + TOP_K)
        + [pl.BlockSpec((tm, LANES), row), _const_spec((1, D_MODEL))],
        out_specs=pl.BlockSpec((tm, D_MODEL), row),
        out_shape=jax.ShapeDtypeStruct((rows, D_MODEL), F32),
        compiler_params=pltpu.CompilerParams(dimension_semantics=("parallel",),
                                             vmem_limit_bytes=VMEM_LIMIT),
        name="combine_final_norm",
    )(h, *yg, tw, g)


def _t5_bucket(dist):
    n = jnp.maximum(dist, 0)
    max_exact = N_BUCKETS // 2
    nf = jnp.maximum(n, 1).astype(F32)
    large = max_exact + (jnp.log(nf / max_exact) / math.log(MAX_DISTANCE / max_exact)
                         * (N_BUCKETS - max_exact)).astype(jnp.int32)
    large = jnp.minimum(large, N_BUCKETS - 1)
    return jnp.where(n < max_exact, n, large)


def _bias_tables(rel_bias, nb):
    table = rel_bias.astype(F32) * LOG2E
    j = jnp.arange(BLK, dtype=jnp.int32)[:, None]
    s = jnp.arange(BLK, dtype=jnp.int32)[None, :]
    from_prev = s > j
    dist = jnp.where(from_prev, j + BLK - s, j - s)

    def lookup(bucket):
        onehot = (bucket[..., None] == jnp.arange(N_BUCKETS, dtype=jnp.int32)).astype(F32)
        return jnp.einsum('...b,bh->...h', onehot, table, precision=lax.Precision.HIGHEST)

    win = lookup(_t5_bucket(dist))
    win = jnp.stack([jnp.where(from_prev[:, :, None], NEG, win), win])
    n_idx = jnp.arange(nb, dtype=jnp.int32)[:, None, None]
    m = jnp.arange(N_META, dtype=jnp.int32)[None, None, :]
    dist_m = N_META + n_idx * BLK + j[None] - m
    meta = lookup(_t5_bucket(dist_m))
    meta = jnp.pad(meta, ((0, 0), (0, 0), (0, LANES - N_META), (0, 0)), constant_values=NEG)
    return win.transpose(0, 3, 1, 2), meta.transpose(0, 3, 1, 2)


def _dup_heads(w):
    w = w.reshape(w.shape[0], N_KV_HEADS, 1, HEAD_DIM)
    return jnp.broadcast_to(w, (w.shape[0], N_KV_HEADS, 2, HEAD_DIM)).reshape(w.shape[0], KV_DUP)


def _pad_lanes(v, value=0.0):
    v = v.reshape(1, -1).astype(F32)
    return jnp.pad(v, ((0, 0), (0, LANES - v.shape[1])), constant_values=value)


def _head_rows(v):
    return jnp.broadcast_to(v.astype(F32)[:, None], (N_SSM_HEADS, BLK))


def _row_tile(rows, target):
    tm = min(rows, target)
    assert rows % tm == 0
    return tm


def kernel(x, meta_tokens, rel_bias, norm_mix_g, w_in, attn_sinks, conv_w, conv_b, dt_bias, a_log,
           d_skip, ssm_norm_g, w_attn_br, w_ssm_br, w_out, norm_ffn_g, w_router, b_router,
           w_exp_in, b_exp_in, w_exp_out, b_exp_out, norm_final_g):
    bsz, seq, _ = x.shape
    assert seq % BLK == 0 and w_in.shape[0] == 1
    nb = seq // BLK
    rows = bsz * seq
    x2d = x.reshape(rows, D_MODEL)

    wi = w_in[0]
    o_k, o_v, o_z = Q_DIM, Q_DIM + KV_DIM, Q_DIM + 2 * KV_DIM
    o_x, o_dt = o_z + D_INNER, o_z + D_INNER + CONV_DIM
    o_g = o_dt + N_SSM_HEADS
    w_proj = jnp.concatenate([
        wi[:, :o_k] * (HEAD_DIM ** -0.5 * LOG2E), _dup_heads(wi[:, o_k:o_v]),
        _dup_heads(wi[:, o_v:o_z]),
        wi[:, o_z:o_x], wi[:, o_x:o_dt], wi[:, o_g:]], axis=1).astype(BF16)
    w_dt = jnp.pad(wi[:, o_dt:o_g], ((0, 0), (0, LANES - N_SSM_HEADS))).astype(BF16)
    g_mix = norm_mix_g[0].reshape(1, D_MODEL)

    tm = _row_tile(rows, 512)
    q, k, v, z, xbc, gate, dtraw = _inproj(x2d, g_mix, w_proj, w_dt, tm)
    _, k_m, v_m, z_m, xbc_m, _, dtraw_m = _inproj(meta_tokens.astype(F32), g_mix, w_proj, w_dt,
                                                  N_META)

    pad_meta = ((BLK - N_META, 0), (0, 0))
    pad_meta_tail = ((0, BLK - N_META), (0, 0))
    bias_win, bias_meta = _bias_tables(rel_bias, nb)
    y_attn = _attention(q, k, v, jnp.pad(k_m, pad_meta_tail), jnp.pad(v_m, pad_meta_tail),
                        bias_win, bias_meta, attn_sinks[0].astype(F32) * LOG2E, bsz, nb)

    ssd_params = (conv_w[0].astype(F32), conv_b[0].reshape(1, CONV_DIM).astype(F32),
                  _head_rows(dt_bias[0]), _head_rows(a_log[0]),
                  jnp.repeat(d_skip[0].astype(F32), SSM_HEAD_DIM).reshape(1, D_INNER),
                  ssm_norm_g[0].reshape(1, D_INNER).astype(F32))
    xbc_mp = jnp.pad(xbc_m, pad_meta)
    zero_state = jnp.zeros((N_SSM_PAIRS, D_STATE, LANES), F32)
    _, h_meta = _ssd(xbc_mp, jnp.pad(z_m, pad_meta), jnp.pad(dtraw_m, pad_meta),
                     jnp.zeros((TAIL, CONV_DIM), BF16), zero_state, *ssd_params, 1, 1,
                     BLK - N_META)
    y_ssm, _ = _ssd(xbc, z, dtraw, xbc_mp[BLK - TAIL:], h_meta[0], *ssd_params, bsz, nb, 0)

    w_r = jnp.pad(w_router[0], ((0, 0), (0, LANES - N_EXPERTS))).astype(BF16)
    b_r = _pad_lanes(b_router[0], NEG)
    h1, u2, top_idx, top_w = _merge(
        y_attn, y_ssm, gate, x2d, w_attn_br[0].astype(BF16), w_ssm_br[0].astype(BF16),
        w_out[0].astype(BF16), norm_ffn_g[0].reshape(1, D_MODEL), w_r, b_r, tm)

    tm_e = 512
    n_tiles = -(-(rows * TOP_K + N_EXPERTS * (tm_e - 1)) // tm_e)
    dest_t, info = _route(top_idx, tm_e, tm)
    counts, pstart, pend = info[0, :N_EXPERTS], info[1, :N_EXPERTS], info[2, :N_EXPERTS]
    tile_start = jnp.arange(n_tiles, dtype=jnp.int32) * tm_e
    tile_e = jnp.minimum(jnp.sum(tile_start[:, None] >= pend[None, :], axis=1),
                         N_EXPERTS - 1).astype(jnp.int32)
    n_valid = pend[-1:] // tm_e
    n_rows = n_tiles * tm_e
    n_assign = rows * TOP_K
    slot = jnp.arange(tm_e, dtype=jnp.int32)[None, :]
    pad_key = (pstart + counts)[:, None] + slot
    pad_key = jnp.where(pad_key < pend[:, None], pad_key, n_rows).reshape(-1)
    tail_key = pend[-1] + jnp.arange(n_rows - n_assign, dtype=jnp.int32)
    tail_key = jnp.where(tail_key < n_rows, tail_key, n_rows)
    n_fill = pad_key.shape[0] + tail_key.shape[0]
    keys = jnp.concatenate([dest_t[:TOP_K].reshape(-1), pad_key, tail_key])
    toks = jnp.concatenate([jnp.tile(jnp.arange(rows, dtype=jnp.int32), TOP_K),
                            jnp.arange(n_fill, dtype=jnp.int32) % rows])
    row_tok = lax.sort((keys, toks), num_keys=1)[1][:n_rows]

    b1 = b_exp_in[0].astype(F32).reshape(N_EXPERTS, 2 * D_FF // FF_CHUNK, LANES, 2)
    b1 = b1.transpose(0, 1, 3, 2).reshape(N_EXPERTS, 1, 2 * D_FF)
    b2 = b_exp_out[0].reshape(N_EXPERTS, 1, D_MODEL).astype(F32)
    y_rows = None
    half = n_tiles // 2
    for t0, t1 in ((0, half), (half, n_tiles)):
        y_rows = _moe(tile_e[t0:t1], jnp.clip(n_valid - t0, 0, t1 - t0),
                      u2[row_tok[t0 * tm_e:t1 * tm_e]], w_exp_in[0], b1, w_exp_out[0], b2, tm_e,
                      t0, n_tiles, y_rows)

    yg = [y_rows[dest_t[k]] for k in range(TOP_K)]
    out = _final(h1, yg, top_w, norm_final_g.reshape(1, D_MODEL).astype(F32), tm)
    return out.reshape(bsz, seq, D_MODEL)
```

```python
import functools
import math

import jax
import jax.numpy as jnp
from jax import lax
from jax.experimental import pallas as pl
from jax.experimental.pallas import tpu as pltpu

F32 = jnp.float32
BF16 = jnp.bfloat16

D_MODEL = 1024
N_META = 16
N_HEADS = 16
N_KV_HEADS = 4
HEAD_DIM = 64
Q_PER_KV = N_HEADS // N_KV_HEADS
Q_DIM = N_HEADS * HEAD_DIM
KV_DIM = N_KV_HEADS * HEAD_DIM
WINDOW = 128
BLK = 128
N_BUCKETS = 32
MAX_DISTANCE = 128
D_INNER = 2 * D_MODEL
SSM_HEAD_DIM = 64
N_SSM_HEADS = D_INNER // SSM_HEAD_DIM
N_GROUPS = 4
D_STATE = 128
CONV_WIDTH = 4
BC_DIM = N_GROUPS * D_STATE
CONV_DIM = D_INNER + 2 * BC_DIM
N_EXPERTS = 32
TOP_K = 4
D_FF = D_MODEL
SWIGLU_ALPHA = 1.702
SWIGLU_LIMIT = 7.0
EPS = 1e-5
NEG = -1e30
LOG2E = math.log2(math.e)

LANES = 128
KV_DUP = 2 * KV_DIM
N_PAIRS = N_HEADS // 2
N_SSM_PAIRS = N_SSM_HEADS // 2
CTX = 3 * BLK
VMEM_LIMIT = 56 * 1024 * 1024

PROJ_SEGS = (("q", Q_DIM), ("k", KV_DUP), ("v", KV_DUP), ("z", D_INNER), ("xbc", CONV_DIM),
             ("gate", 2 * D_MODEL))
PROJ_N = sum(w for _, w in PROJ_SEGS)
PROJ_CHUNK = 512


def _dot(a, b):
    return jnp.dot(a, b, preferred_element_type=F32)


def _dot_nt(a, b):
    return lax.dot_general(a, b, (((1,), (1,)), ((), ())), preferred_element_type=F32)


def _sigmoid(x):
    return 0.5 * jnp.tanh(0.5 * x) + 0.5


def _silu(x):
    h = 0.5 * x
    return h * jnp.tanh(h) + h


def _const_spec(shape):
    nd = len(shape)
    return pl.BlockSpec(shape, lambda *_: (0,) * nd, pipeline_mode=pl.Buffered(1))


def _inproj_kernel(x_ref, g_ref, w_ref, wdt_ref, q_ref, k_ref, v_ref, z_ref, xbc_ref, gate_ref,
                   dt_ref):
    x = x_ref[...]
    ms = jnp.mean(x * x, axis=-1, keepdims=True)
    u = (x * lax.rsqrt(ms + EPS) * g_ref[...]).astype(BF16)
    outs = (q_ref, k_ref, v_ref, z_ref, xbc_ref, gate_ref)
    off = 0
    for ref, (_, width) in zip(outs, PROJ_SEGS):
        for c0 in range(0, width, PROJ_CHUNK):
            ref[:, c0:c0 + PROJ_CHUNK] = _dot(
                u, w_ref[:, off + c0:off + c0 + PROJ_CHUNK]).astype(BF16)
        off += width
    dt_ref[...] = _dot(u, wdt_ref[...])


def _inproj(x2d, g, w, wdt, tm):
    rows = x2d.shape[0]
    row = lambda i: (i, 0)
    out_shape = [jax.ShapeDtypeStruct((rows, wd), BF16) for _, wd in PROJ_SEGS]
    out_shape.append(jax.ShapeDtypeStruct((rows, LANES), F32))
    out_specs = [pl.BlockSpec((tm, wd), row) for _, wd in PROJ_SEGS]
    out_specs.append(pl.BlockSpec((tm, LANES), row))
    return pl.pallas_call(
        _inproj_kernel,
        grid=(rows // tm,),
        in_specs=[pl.BlockSpec((tm, D_MODEL), row), _const_spec((1, D_MODEL)),
                  _const_spec((D_MODEL, PROJ_N)), _const_spec((D_MODEL, LANES))],
        out_specs=out_specs,
        out_shape=out_shape,
        compiler_params=pltpu.CompilerParams(dimension_semantics=("parallel",),
                                             vmem_limit_bytes=VMEM_LIMIT),
        name="inproj",
    )(x2d, g, w, wdt)


def _attn_kernel(sink_ref, q_ref, kp_ref, kc_ref, km_ref, vp_ref, vc_ref, vm_ref, bwin_ref,
                 bmeta_ref, o_ref, s_ref, p_ref, inv_ref):
    lo = lax.broadcasted_iota(jnp.int32, (1, LANES), 1) < HEAD_DIM
    from_prev = (lax.broadcasted_iota(jnp.int32, (BLK, BLK), 1)
                 > lax.broadcasted_iota(jnp.int32, (BLK, BLK), 0))
    zero = jnp.zeros((), BF16)
    pairs_per_kv = Q_PER_KV // 2

    for g in range(N_KV_HEADS):
        cs = slice(g * LANES, (g + 1) * LANES)
        kk = jnp.concatenate([kp_ref[:, cs], kc_ref[:, cs], km_ref[:, cs]], axis=0)
        k_lo = jnp.where(lo, kk, zero)
        k_hi = jnp.where(lo, zero, kk)
        for j in range(pairs_per_kv):
            pi = g * pairs_per_kv + j
            qp = q_ref[:, pi * LANES:(pi + 1) * LANES]
            s_ref[2 * pi] = _dot_nt(qp, k_lo)
            s_ref[2 * pi + 1] = _dot_nt(qp, k_hi)

    for pi in range(N_PAIRS):
        invs = []
        for t in range(2):
            h = 2 * pi + t
            s_win = (jnp.where(from_prev, s_ref[h, :, :BLK], s_ref[h, :, BLK:2 * BLK])
                     + bwin_ref[0, h])
            s_meta = s_ref[h, :, 2 * BLK:] + bmeta_ref[0, h]
            sink = sink_ref[h]
            m = jnp.maximum(jnp.max(jnp.maximum(s_win, s_meta), axis=-1, keepdims=True), sink)
            p_win = jnp.exp2(s_win - m)
            p_meta = jnp.exp2(s_meta - m)
            den = jnp.sum(p_win + p_meta, axis=-1, keepdims=True) + jnp.exp2(sink - m)
            c0 = t * CTX
            p_ref[pi, :, c0:c0 + BLK] = jnp.where(from_prev, p_win, 0.0).astype(BF16)
            p_ref[pi, :, c0 + BLK:c0 + 2 * BLK] = jnp.where(from_prev, 0.0, p_win).astype(BF16)
            p_ref[pi, :, c0 + 2 * BLK:c0 + CTX] = p_meta.astype(BF16)
            invs.append(1.0 / den)
        inv_ref[pi] = jnp.where(lo, invs[0], invs[1])

    for g in range(N_KV_HEADS):
        cs = slice(g * LANES, (g + 1) * LANES)
        vv = jnp.concatenate([vp_ref[:, cs], vc_ref[:, cs], vm_ref[:, cs]], axis=0)
        v_bd = jnp.concatenate([jnp.where(lo, vv, zero), jnp.where(lo, zero, vv)], axis=0)
        for j in range(pairs_per_kv):
            pi = g * pairs_per_kv + j
            o = _dot(p_ref[pi], v_bd) * inv_ref[pi]
            o_ref[:, pi * LANES:(pi + 1) * LANES] = o.astype(BF16)


def _attention(q, k, v, k_meta, v_meta, bias_win, bias_meta, sinks, bsz, nb):
    cur = lambda b, n, *_: (b * nb + n, 0)
    prev = lambda b, n, *_: (jnp.maximum(b * nb + n - 1, 0), 0)
    const2 = lambda b, n, *_: (0, 0)
    grid_spec = pltpu.PrefetchScalarGridSpec(
        num_scalar_prefetch=1,
        grid=(bsz, nb),
        in_specs=[
            pl.BlockSpec((BLK, Q_DIM), cur),
            pl.BlockSpec((BLK, KV_DUP), prev), pl.BlockSpec((BLK, KV_DUP), cur),
            pl.BlockSpec((BLK, KV_DUP), const2),
            pl.BlockSpec((BLK, KV_DUP), prev), pl.BlockSpec((BLK, KV_DUP), cur),
            pl.BlockSpec((BLK, KV_DUP), const2),
            pl.BlockSpec((1, N_HEADS, BLK, BLK), lambda b, n, *_: (jnp.minimum(n, 1), 0, 0, 0)),
            pl.BlockSpec((1, N_HEADS, BLK, LANES), lambda b, n, *_: (n, 0, 0, 0)),
        ],
        out_specs=pl.BlockSpec((BLK, Q_DIM), cur),
        scratch_shapes=[pltpu.VMEM((N_HEADS, BLK, CTX), F32),
                        pltpu.VMEM((N_PAIRS, BLK, 2 * CTX), BF16),
                        pltpu.VMEM((N_PAIRS, BLK, LANES), F32)],
    )
    return pl.pallas_call(
        _attn_kernel,
        grid_spec=grid_spec,
        out_shape=jax.ShapeDtypeStruct((bsz * nb * BLK, Q_DIM), BF16),
        compiler_params=pltpu.CompilerParams(dimension_semantics=("parallel", "arbitrary"),
                                             vmem_limit_bytes=VMEM_LIMIT),
        name="swa_attention",
    )(sinks, q, k, k, k_meta, v, v, v_meta, bias_win, bias_meta)


CONV_COLS = 256
TAIL = 16


def _ssd_kernel(xbc_ref, z_ref, dtraw_ref, tail0_ref, h0_ref, convw_ref, convb_ref, dtb_ref,
                alog_ref, dskip_ref, ng_ref, y_ref, hfin_ref, win_ref, xc_ref, st_ref, *, n_pad):
    for s in range(xbc_ref.shape[0]):
        _ssd_chunk(xbc_ref.at[s], z_ref.at[s], dtraw_ref.at[s], tail0_ref, h0_ref, convw_ref,
                   convb_ref, dtb_ref, alog_ref, dskip_ref, ng_ref, y_ref.at[s], hfin_ref.at[s],
                   win_ref.at[s], xc_ref.at[s], st_ref.at[s], n_pad)


def _ssd_chunk(xbc_ref, z_ref, dtraw_ref, tail0_ref, h0_ref, convw_ref, convb_ref, dtb_ref,
               alog_ref, dskip_ref, ng_ref, y_ref, hfin_ref, win_ref, xc_ref, st_ref, n_pad):
    c = pl.program_id(1)

    @pl.when(c == 0)
    def _():
        win_ref[0:TAIL, :] = tail0_ref[...]
        st_ref[...] = h0_ref[...]

    win_ref[TAIL:TAIL + BLK, :] = xbc_ref[...]
    sr = lax.broadcasted_iota(jnp.int32, (BLK, TAIL + BLK), 0)
    sc = lax.broadcasted_iota(jnp.int32, (BLK, TAIL + BLK), 1)
    shifts = [jnp.where(sc == sr + (TAIL - (CONV_WIDTH - 1) + w), 1.0, 0.0).astype(BF16)
              for w in range(CONV_WIDTH)]
    live = lax.broadcasted_iota(jnp.int32, (BLK, 1), 0) >= n_pad
    for c0 in range(0, CONV_DIM, CONV_COLS):
        cs = slice(c0, c0 + CONV_COLS)
        wv = win_ref[:, cs]
        acc = convb_ref[:, cs]
        for w in range(CONV_WIDTH):
            acc = acc + convw_ref[w:w + 1, cs] * _dot(shifts[w], wv)
        act = _silu(acc)
        if n_pad:
            act = jnp.where(live, act, 0.0)
        xc_ref[:, cs] = act
    win_ref[0:TAIL, :] = win_ref[BLK:BLK + TAIL, :]

    lane_t = lax.broadcasted_iota(jnp.int32, (N_SSM_HEADS, BLK), 1)
    x_t = dtraw_ref[...].T[:N_SSM_HEADS] + dtb_ref[...]
    dt_t = jnp.maximum(x_t, 0.0) + jnp.log1p(jnp.exp(-jnp.abs(x_t)))
    if n_pad:
        dt_t = jnp.where(lane_t >= n_pad, dt_t, 0.0)
    a_cs_t = dt_t * (-jnp.exp(alog_ref[...]))
    shift = 1
    while shift < BLK:
        a_cs_t = a_cs_t + jnp.where(lane_t >= shift, pltpu.roll(a_cs_t, shift, 1), 0.0)
        shift *= 2
    w_t = dt_t * jnp.exp(a_cs_t[:, BLK - 1:BLK] - a_cs_t)
    src_t = a_cs_t - jnp.log(dt_t)
    fill = jnp.zeros((BLK - N_SSM_HEADS, BLK), F32)
    a_cs = jnp.concatenate([a_cs_t, fill], axis=0).T
    e_cs = jnp.exp(a_cs)
    ri = lax.broadcasted_iota(jnp.int32, (BLK, BLK), 0)
    ci = lax.broadcasted_iota(jnp.int32, (BLK, BLK), 1)
    causal = ri >= ci
    lo = lax.broadcasted_iota(jnp.int32, (1, LANES), 1) < SSM_HEAD_DIM

    pairs_per_group = N_SSM_PAIRS // N_GROUPS
    for g in range(N_GROUPS):
        b_g = xc_ref[:, D_INNER + g * D_STATE:D_INNER + (g + 1) * D_STATE]
        c_g = xc_ref[:, D_INNER + BC_DIM + g * D_STATE:D_INNER + BC_DIM + (g + 1) * D_STATE]
        cb = _dot_nt(c_g.astype(BF16), b_g.astype(BF16))
        b_gt = b_g.T
        ys = []
        for j in range(pairs_per_group):
            pi = g * pairs_per_group + j
            ps = slice(pi * LANES, (pi + 1) * LANES)
            xs = xc_ref[:, ps]
            xs_b = xs.astype(BF16)
            st = st_ref[pi]
            rhs = jnp.concatenate([xs_b, st.astype(BF16)], axis=0)
            y_h, st_h = [], []
            for t in range(2):
                h = 2 * pi + t
                diff = a_cs[:, h:h + 1] - src_t[h:h + 1, :]
                m_h = cb * jnp.exp(jnp.where(causal, diff, -jnp.inf))
                ce = c_g * e_cs[:, h:h + 1]
                lhs = jnp.concatenate([m_h, ce], axis=1).astype(BF16)
                y_h.append(_dot(lhs, rhs))
                s_new = _dot((b_gt * w_t[h:h + 1, :]).astype(BF16), xs_b)
                st_h.append(st * e_cs[BLK - 1:BLK, h:h + 1] + s_new)
            st_ref[pi] = jnp.where(lo, st_h[0], st_h[1])
            ys.append(jnp.where(lo, y_h[0], y_h[1]) + dskip_ref[:, ps] * xs)
        gs = slice(g * (D_INNER // N_GROUPS), (g + 1) * (D_INNER // N_GROUPS))
        zg = z_ref[:, gs].astype(F32)
        yg = jnp.concatenate(ys, axis=1) * _silu(zg)
        ms = jnp.mean(yg * yg, axis=-1, keepdims=True)
        y_ref[:, gs] = (yg * lax.rsqrt(ms + EPS) * ng_ref[:, gs]).astype(BF16)

    @pl.when(c == pl.num_programs(1) - 1)
    def _():
        hfin_ref[0] = st_ref[...]


def _ssd(xbc, z, dtraw, tail0, h0, convw, convb, dtb, alog, dskip, ng, bsz, nc, n_pad):
    n_seq = 2 if bsz % 2 == 0 else 1
    nbg = bsz // n_seq
    rows_g = nbg * nc * BLK
    cur = lambda b, c: (0, b * nc + c, 0)
    split = lambda a: a.reshape(n_seq, rows_g, a.shape[-1])
    y, h_fin = pl.pallas_call(
        functools.partial(_ssd_kernel, n_pad=n_pad),
        grid=(nbg, nc),
        in_specs=[
            pl.BlockSpec((n_seq, BLK, CONV_DIM), cur), pl.BlockSpec((n_seq, BLK, D_INNER), cur),
            pl.BlockSpec((n_seq, BLK, LANES), cur),
            _const_spec((TAIL, CONV_DIM)), _const_spec((N_SSM_PAIRS, D_STATE, LANES)),
            _const_spec((CONV_WIDTH, CONV_DIM)), _const_spec((1, CONV_DIM)),
            _const_spec((N_SSM_HEADS, BLK)), _const_spec((N_SSM_HEADS, BLK)),
            _const_spec((1, D_INNER)), _const_spec((1, D_INNER)),
        ],
        out_specs=[pl.BlockSpec((n_seq, BLK, D_INNER), cur),
                   pl.BlockSpec((n_seq, 1, N_SSM_PAIRS, D_STATE, LANES),
                                lambda b, c: (0, b, 0, 0, 0))],
        out_shape=[jax.ShapeDtypeStruct((n_seq, rows_g, D_INNER), BF16),
                   jax.ShapeDtypeStruct((n_seq, nbg, N_SSM_PAIRS, D_STATE, LANES), F32)],
        scratch_shapes=[pltpu.VMEM((n_seq, TAIL + BLK, CONV_DIM), BF16),
                        pltpu.VMEM((n_seq, BLK, CONV_DIM), F32),
                        pltpu.VMEM((n_seq, N_SSM_PAIRS, D_STATE, LANES), F32)],
        compiler_params=pltpu.CompilerParams(dimension_semantics=("parallel", "arbitrary"),
                                             vmem_limit_bytes=VMEM_LIMIT),
        name="ssd_mixer",
    )(split(xbc), split(z), split(dtraw), tail0, h0, convw, convb, dtb, alog, dskip, ng)
    return (y.reshape(bsz * nc * BLK, D_INNER),
            h_fin.reshape(bsz, N_SSM_PAIRS, D_STATE, LANES))


def _merge_kernel(ya_ref, ys_ref, gl_ref, x_ref, wa_ref, ws_ref, wo_ref, g_ref, wr_ref, br_ref,
                  h_ref, u_ref, idx_ref, tw_ref, cnt_ref):
    g0 = _sigmoid(gl_ref[:, :D_MODEL].astype(F32))
    g1 = _sigmoid(gl_ref[:, D_MODEL:].astype(F32))
    merged = g0 * _dot(ya_ref[...], wa_ref[...]) + g1 * _dot(ys_ref[...], ws_ref[...])
    h = x_ref[...] + _dot(merged.astype(BF16), wo_ref[...])
    h_ref[...] = h
    ms = jnp.mean(h * h, axis=-1, keepdims=True)
    u = (h * lax.rsqrt(ms + EPS) * g_ref[...]).astype(BF16)
    u_ref[...] = u
    logits = _dot(u, wr_ref[...]) + br_ref[...]
    lane = lax.broadcasted_iota(jnp.int32, logits.shape, 1).astype(F32)
    idx_out = jnp.zeros(logits.shape, F32)
    val_out = jnp.zeros(logits.shape, F32)
    chosen = jnp.zeros(logits.shape, F32)
    top = None
    den = None
    for k in range(TOP_K):
        m = jnp.max(logits, axis=-1, keepdims=True)
        sel = jnp.min(jnp.where(logits == m, lane, float(LANES)), axis=-1, keepdims=True)
        if k == 0:
            top = m
        e = jnp.exp(m - top)
        den = e if k == 0 else den + e
        idx_out = jnp.where(lane == k, sel, idx_out)
        val_out = jnp.where(lane == k, e, val_out)
        chosen = jnp.where(lane == sel, 1.0, chosen)
        logits = jnp.where(lane == sel, -jnp.inf, logits)
    idx_ref[...] = idx_out.astype(jnp.int32)
    tw_ref[...] = val_out / den

    @pl.when(pl.program_id(0) == 0)
    def _():
        cnt_ref[...] = jnp.zeros_like(cnt_ref)

    cnt_ref[...] += jnp.sum(chosen, axis=0, keepdims=True)


def _merge(ya, ys, gl, x2d, wa, ws, wo, g, wr, br, tm):
    rows = x2d.shape[0]
    row = lambda i: (i, 0)
    return pl.pallas_call(
        _merge_kernel,
        grid=(rows // tm,),
        in_specs=[pl.BlockSpec((tm, Q_DIM), row), pl.BlockSpec((tm, D_INNER), row),
                  pl.BlockSpec((tm, 2 * D_MODEL), row), pl.BlockSpec((tm, D_MODEL), row),
                  _const_spec((Q_DIM, D_MODEL)), _const_spec((D_INNER, D_MODEL)),
                  _const_spec((D_MODEL, D_MODEL)), _const_spec((1, D_MODEL)),
                  _const_spec((D_MODEL, LANES)), _const_spec((1, LANES))],
        out_specs=[pl.BlockSpec((tm, D_MODEL), row), pl.BlockSpec((tm, D_MODEL), row),
                   pl.BlockSpec((tm, LANES), row), pl.BlockSpec((tm, LANES), row),
                   pl.BlockSpec((8, LANES), lambda i: (0, 0))],
        out_shape=[jax.ShapeDtypeStruct((rows, D_MODEL), F32),
                   jax.ShapeDtypeStruct((rows, D_MODEL), BF16),
                   jax.ShapeDtypeStruct((rows, LANES), jnp.int32),
                   jax.ShapeDtypeStruct((rows, LANES), F32),
                   jax.ShapeDtypeStruct((8, LANES), F32)],
        compiler_params=pltpu.CompilerParams(dimension_semantics=("arbitrary",),
                                             vmem_limit_bytes=VMEM_LIMIT),
        name="merge_router",
    )(ya, ys, gl, x2d, wa, ws, wo, g, wr, br)


def _route_kernel(idx_ref, cnt_ref, dest_ref, info_ref, base_ref, *, tm_e):
    i = pl.program_id(0)
    tr = idx_ref.shape[0]
    lane = lax.broadcasted_iota(jnp.int32, (tr, LANES), 1)
    idx = idx_ref[...]
    onehot = [jnp.where(lane == idx[:, k:k + 1], 1.0, 0.0) for k in range(TOP_K)]
    oh_all = onehot[0] + onehot[1] + onehot[2] + onehot[3]

    @pl.when(i == 0)
    def _():
        counts = cnt_ref[...]
        padded = jnp.floor((counts + (tm_e - 1)) * (1.0 / tm_e)) * tm_e
        lane8 = lax.broadcasted_iota(jnp.int32, (8, LANES), 1)
        incl = padded
        shift = 1
        while shift < LANES:
            incl = incl + jnp.where(lane8 >= shift, pltpu.roll(incl, shift, 1), 0.0)
            shift *= 2
        pstart = incl - padded
        base_ref[...] = pstart[0:1]
        row8 = lax.broadcasted_iota(jnp.int32, (8, LANES), 0)
        info = jnp.where(row8 == 0, counts, jnp.where(row8 == 1, pstart, incl))
        info_ref[...] = info.astype(jnp.int32)

    r = lax.broadcasted_iota(jnp.int32, (tr, tr), 0)
    c = lax.broadcasted_iota(jnp.int32, (tr, tr), 1)
    before = jnp.where(r > c, 1.0, 0.0).astype(BF16)
    earlier = _dot(before, oh_all.astype(BF16))
    rank = base_ref[...] + earlier
    dest = jnp.zeros((tr, LANES), F32)
    for k in range(TOP_K):
        d_k = jnp.sum(onehot[k] * rank, axis=-1, keepdims=True)
        dest = jnp.where(lane == k, d_k, dest)
    dest_ref[...] = dest.T[0:8].astype(jnp.int32)
    base_ref[...] += jnp.sum(oh_all, axis=0, keepdims=True)


def _route(top_idx, counts, tm_e, tr):
    rows = top_idx.shape[0]
    return pl.pallas_call(
        functools.partial(_route_kernel, tm_e=tm_e),
        grid=(rows // tr,),
        in_specs=[pl.BlockSpec((tr, LANES), lambda i: (i, 0)), _const_spec((8, LANES))],
        out_specs=[pl.BlockSpec((8, tr), lambda i: (0, i)),
                   pl.BlockSpec((8, LANES), lambda i: (0, 0))],
        out_shape=[jax.ShapeDtypeStruct((8, rows), jnp.int32),
                   jax.ShapeDtypeStruct((8, LANES), jnp.int32)],
        scratch_shapes=[pltpu.VMEM((1, LANES), F32)],
        compiler_params=pltpu.CompilerParams(dimension_semantics=("arbitrary",),
                                             vmem_limit_bytes=VMEM_LIMIT),
        name="route",
    )(top_idx, counts)


FF_CHUNK = 2 * LANES


def _moe_kernel(te_ref, nv_ref, x_ref, w1_ref, b1_ref, w2_ref, b2_ref, y_ref, w1s_ref, w2s_ref):
    i = pl.program_id(0)
    valid = i < nv_ref[0]
    fresh = jnp.logical_or(i == 0, te_ref[i] != te_ref[jnp.maximum(i - 1, 0)])

    @pl.when(jnp.logical_and(valid, fresh))
    def _():
        r = lax.broadcasted_iota(jnp.int32, (FF_CHUNK, FF_CHUNK), 0)
        c = lax.broadcasted_iota(jnp.int32, (FF_CHUNK, FF_CHUNK), 1)
        src = jnp.where(c < LANES, 2 * c, 2 * (c - LANES) + 1)
        perm = jnp.where(r == src, 1.0, 0.0).astype(BF16)
        for c0 in range(0, 2 * D_FF, FF_CHUNK):
            cs = slice(c0, c0 + FF_CHUNK)
            w1s_ref[:, cs] = _dot(w1_ref[0, :, cs].astype(BF16), perm).astype(BF16)
        w2s_ref[...] = w2_ref[0].astype(BF16)

    @pl.when(valid)
    def _():
        x = x_ref[...]
        acts = []
        for c0 in range(0, 2 * D_FF, FF_CHUNK):
            cs = slice(c0, c0 + FF_CHUNK)
            hh = _dot(x, w1s_ref[:, cs]) + b1_ref[0, :, cs]
            glu = jnp.minimum(hh[:, :LANES], SWIGLU_LIMIT)
            lin = jnp.clip(hh[:, LANES:], -SWIGLU_LIMIT, SWIGLU_LIMIT)
            acts.append((glu * _sigmoid(SWIGLU_ALPHA * glu) * (lin + 1.0)).astype(BF16))
        act = jnp.concatenate(acts, axis=1)
        y_ref[...] = (_dot(act, w2s_ref[...]) + b2_ref[0]).astype(BF16)

    @pl.when(jnp.logical_not(valid))
    def _():
        y_ref[...] = jnp.zeros_like(y_ref)


def _moe_kernel_into(te_ref, nv_ref, x_ref, w1_ref, b1_ref, w2_ref, b2_ref, yprev_ref, y_ref,
                     w1s_ref, w2s_ref):
    del yprev_ref
    _moe_kernel(te_ref, nv_ref, x_ref, w1_ref, b1_ref, w2_ref, b2_ref, y_ref, w1s_ref, w2s_ref)


def _moe(tile_e, n_valid, xrows, w1, b1, w2, b2, tm, tile0, n_tiles_total, y_prev=None):
    n_tiles = xrows.shape[0] // tm
    wsel = lambda i, te, nv: (te[i], 0, 0)
    in_specs = [pl.BlockSpec((tm, D_MODEL), lambda i, te, nv: (i, 0)),
                pl.BlockSpec((1, D_MODEL, 2 * D_FF), wsel), pl.BlockSpec((1, 1, 2 * D_FF), wsel),
                pl.BlockSpec((1, D_FF, D_MODEL), wsel), pl.BlockSpec((1, 1, D_MODEL), wsel)]
    args = [tile_e, n_valid, xrows, w1, b1, w2, b2]
    if y_prev is not None:
        in_specs.append(pl.BlockSpec(memory_space=pl.ANY))
        args.append(y_prev)
    grid_spec = pltpu.PrefetchScalarGridSpec(
        num_scalar_prefetch=2,
        grid=(n_tiles,),
        in_specs=in_specs,
        out_specs=pl.BlockSpec((tm, D_MODEL), lambda i, te, nv: (i + tile0, 0)),
        scratch_shapes=[pltpu.VMEM((D_MODEL, 2 * D_FF), BF16), pltpu.VMEM((D_FF, D_MODEL), BF16)],
    )
    return pl.pallas_call(
        _moe_kernel if y_prev is None else _moe_kernel_into,
        grid_spec=grid_spec,
        out_shape=jax.ShapeDtypeStruct((n_tiles_total * tm, D_MODEL), BF16),
        input_output_aliases={} if y_prev is None else {len(args) - 1: 0},
        compiler_params=pltpu.CompilerParams(dimension_semantics=("arbitrary",),
                                             vmem_limit_bytes=VMEM_LIMIT),
        name="moe_experts",
    )(*args)


def _final_kernel(h_ref, y0_ref, y1_ref, y2_ref, y3_ref, tw_ref, g_ref, o_ref):
    h = h_ref[...]
    tw = tw_ref[...]
    for k, y_ref in enumerate((y0_ref, y1_ref, y2_ref, y3_ref)):
        h = h + tw[:, k:k + 1] * y_ref[...].astype(F32)
    ms = jnp.mean(h * h, axis=-1, keepdims=True)
    o_ref[...] = h * lax.rsqrt(ms + EPS) * g_ref[...]


def _final(h, yg, tw, g, tm):
    rows = h.shape[0]
    row = lambda i: (i, 0)
    return pl.pallas_call(
        _final_kernel,
        grid=(rows // tm,),
        in_specs=[pl.BlockSpec((tm, D_MODEL), row)] * (1 + TOP_K)
        + [pl.BlockSpec((tm, LANES), row), _const_spec((1, D_MODEL))],
        out_specs=pl.BlockSpec((tm, D_MODEL), row),
        out_shape=jax.ShapeDtypeStruct((rows, D_MODEL), F32),
        compiler_params=pltpu.CompilerParams(dimension_semantics=("parallel",),
                                             vmem_limit_bytes=VMEM_LIMIT),
        name="combine_final_norm",
    )(h, *yg, tw, g)


def _t5_bucket(dist):
    n = jnp.maximum(dist, 0)
    max_exact = N_BUCKETS // 2
    nf = jnp.maximum(n, 1).astype(F32)
    large = max_exact + (jnp.log(nf / max_exact) / math.log(MAX_DISTANCE / max_exact)
                         * (N_BUCKETS - max_exact)).astype(jnp.int32)
    large = jnp.minimum(large, N_BUCKETS - 1)
    return jnp.where(n < max_exact, n, large)


def _bias_tables(rel_bias, nb):
    table = rel_bias.astype(F32) * LOG2E
    j = jnp.arange(BLK, dtype=jnp.int32)[:, None]
    s = jnp.arange(BLK, dtype=jnp.int32)[None, :]
    from_prev = s > j
    dist = jnp.where(from_prev, j + BLK - s, j - s)

    def lookup(bucket):
        onehot = (bucket[..., None] == jnp.arange(N_BUCKETS, dtype=jnp.int32)).astype(F32)
        return jnp.einsum('...b,bh->...h', onehot, table, precision=lax.Precision.HIGHEST)

    win = lookup(_t5_bucket(dist))
    win = jnp.stack([jnp.where(from_prev[:, :, None], NEG, win), win])
    n_idx = jnp.arange(nb, dtype=jnp.int32)[:, None, None]
    m = jnp.arange(N_META, dtype=jnp.int32)[None, None, :]
    dist_m = N_META + n_idx * BLK + j[None] - m
    meta = lookup(_t5_bucket(dist_m))
    meta = jnp.pad(meta, ((0, 0), (0, 0), (0, LANES - N_META), (0, 0)), constant_values=NEG)
    return win.transpose(0, 3, 1, 2), meta.transpose(0, 3, 1, 2)


def _dup_heads(w):
    w = w.reshape(w.shape[0], N_KV_HEADS, 1, HEAD_DIM)
    return jnp.broadcast_to(w, (w.shape[0], N_KV_HEADS, 2, HEAD_DIM)).reshape(w.shape[0], KV_DUP)


def _pad_lanes(v, value=0.0):
    v = v.reshape(1, -1).astype(F32)
    return jnp.pad(v, ((0, 0), (0, LANES - v.shape[1])), constant_values=value)


def _head_rows(v):
    return jnp.broadcast_to(v.astype(F32)[:, None], (N_SSM_HEADS, BLK))


def _row_tile(rows, target):
    tm = min(rows, target)
    assert rows % tm == 0
    return tm


def kernel(x, meta_tokens, rel_bias, norm_mix_g, w_in, attn_sinks, conv_w, conv_b, dt_bias, a_log,
           d_skip, ssm_norm_g, w_attn_br, w_ssm_br, w_out, norm_ffn_g, w_router, b_router,
           w_exp_in, b_exp_in, w_exp_out, b_exp_out, norm_final_g):
    bsz, seq, _ = x.shape
    assert seq % BLK == 0 and w_in.shape[0] == 1
    nb = seq // BLK
    rows = bsz * seq
    x2d = x.reshape(rows, D_MODEL)

    wi = w_in[0]
    o_k, o_v, o_z = Q_DIM, Q_DIM + KV_DIM, Q_DIM + 2 * KV_DIM
    o_x, o_dt = o_z + D_INNER, o_z + D_INNER + CONV_DIM
    o_g = o_dt + N_SSM_HEADS
    w_proj = jnp.concatenate([
        wi[:, :o_k] * (HEAD_DIM ** -0.5 * LOG2E), _dup_heads(wi[:, o_k:o_v]),
        _dup_heads(wi[:, o_v:o_z]),
        wi[:, o_z:o_x], wi[:, o_x:o_dt], wi[:, o_g:]], axis=1).astype(BF16)
    w_dt = jnp.pad(wi[:, o_dt:o_g], ((0, 0), (0, LANES - N_SSM_HEADS))).astype(BF16)
    g_mix = norm_mix_g[0].reshape(1, D_MODEL)

    tm = _row_tile(rows, 512)
    q, k, v, z, xbc, gate, dtraw = _inproj(x2d, g_mix, w_proj, w_dt, tm)
    _, k_m, v_m, z_m, xbc_m, _, dtraw_m = _inproj(meta_tokens.astype(F32), g_mix, w_proj, w_dt,
                                                  N_META)

    pad_meta = ((BLK - N_META, 0), (0, 0))
    pad_meta_tail = ((0, BLK - N_META), (0, 0))
    bias_win, bias_meta = _bias_tables(rel_bias, nb)
    y_attn = _attention(q, k, v, jnp.pad(k_m, pad_meta_tail), jnp.pad(v_m, pad_meta_tail),
                        bias_win, bias_meta, attn_sinks[0].astype(F32) * LOG2E, bsz, nb)

    ssd_params = (conv_w[0].astype(F32), conv_b[0].reshape(1, CONV_DIM).astype(F32),
                  _head_rows(dt_bias[0]), _head_rows(a_log[0]),
                  jnp.repeat(d_skip[0].astype(F32), SSM_HEAD_DIM).reshape(1, D_INNER),
                  ssm_norm_g[0].reshape(1, D_INNER).astype(F32))
    xbc_mp = jnp.pad(xbc_m, pad_meta)
    zero_state = jnp.zeros((N_SSM_PAIRS, D_STATE, LANES), F32)
    _, h_meta = _ssd(xbc_mp, jnp.pad(z_m, pad_meta), jnp.pad(dtraw_m, pad_meta),
                     jnp.zeros((TAIL, CONV_DIM), BF16), zero_state, *ssd_params, 1, 1,
                     BLK - N_META)
    y_ssm, _ = _ssd(xbc, z, dtraw, xbc_mp[BLK - TAIL:], h_meta[0], *ssd_params, bsz, nb, 0)

    w_r = jnp.pad(w_router[0], ((0, 0), (0, LANES - N_EXPERTS))).astype(BF16)
    b_r = _pad_lanes(b_router[0], NEG)
    h1, u2, top_idx, top_w, counts8 = _merge(
        y_attn, y_ssm, gate, x2d, w_attn_br[0].astype(BF16), w_ssm_br[0].astype(BF16),
        w_out[0].astype(BF16), norm_ffn_g[0].reshape(1, D_MODEL), w_r, b_r, tm)

    tm_e = 512
    n_tiles = -(-(rows * TOP_K + N_EXPERTS * (tm_e - 1)) // tm_e)
    dest_t, info = _route(top_idx, counts8, tm_e, _row_tile(rows, 1024))
    counts, pstart, pend = info[0, :N_EXPERTS], info[1, :N_EXPERTS], info[2, :N_EXPERTS]
    tile_start = jnp.arange(n_tiles, dtype=jnp.int32) * tm_e
    tile_e = jnp.minimum(jnp.sum(tile_start[:, None] >= pend[None, :], axis=1),
                         N_EXPERTS - 1).astype(jnp.int32)
    n_valid = pend[-1:] // tm_e
    n_rows = n_tiles * tm_e
    n_assign = rows * TOP_K
    slot = jnp.arange(tm_e, dtype=jnp.int32)[None, :]
    pad_key = (pstart + counts)[:, None] + slot
    pad_key = jnp.where(pad_key < pend[:, None], pad_key, n_rows).reshape(-1)
    tail_key = pend[-1] + jnp.arange(n_rows - n_assign, dtype=jnp.int32)
    tail_key = jnp.where(tail_key < n_rows, tail_key, n_rows)
    n_fill = pad_key.shape[0] + tail_key.shape[0]
    keys = jnp.concatenate([dest_t[:TOP_K].reshape(-1), pad_key, tail_key])
    toks = jnp.concatenate([jnp.tile(jnp.arange(rows, dtype=jnp.int32), TOP_K),
                            jnp.arange(n_fill, dtype=jnp.int32) % rows])
    row_tok = lax.sort((keys, toks), num_keys=1)[1][:n_rows]

    b1 = b_exp_in[0].astype(F32).reshape(N_EXPERTS, 2 * D_FF // FF_CHUNK, LANES, 2)
    b1 = b1.transpose(0, 1, 3, 2).reshape(N_EXPERTS, 1, 2 * D_FF)
    b2 = b_exp_out[0].reshape(N_EXPERTS, 1, D_MODEL).astype(F32)
    y_rows = None
    n_parts = 4
    bounds = [n_tiles * p // n_parts for p in range(n_parts + 1)]
    for t0, t1 in zip(bounds[:-1], bounds[1:]):
        y_rows = _moe(tile_e[t0:t1], jnp.clip(n_valid - t0, 0, t1 - t0),
                      u2[row_tok[t0 * tm_e:t1 * tm_e]], w_exp_in[0], b1, w_exp_out[0], b2, tm_e,
                      t0, n_tiles, y_rows)

    yg = [y_rows[dest_t[k]] for k in range(TOP_K)]
    out = _final(h1, yg, top_w, norm_final_g.reshape(1, D_MODEL).astype(F32), tm)
    return out.reshape(bsz, seq, D_MODEL)
```

```python
import functools
import math

import jax
import jax.numpy as jnp
from jax import lax
from jax.experimental import pallas as pl
from jax.experimental.pallas import tpu as pltpu

F32 = jnp.float32
BF16 = jnp.bfloat16

D_MODEL = 1024
N_META = 16
N_HEADS = 16
N_KV_HEADS = 4
HEAD_DIM = 64
Q_PER_KV = N_HEADS // N_KV_HEADS
Q_DIM = N_HEADS * HEAD_DIM
KV_DIM = N_KV_HEADS * HEAD_DIM
WINDOW = 128
BLK = 128
N_BUCKETS = 32
MAX_DISTANCE = 128
D_INNER = 2 * D_MODEL
SSM_HEAD_DIM = 64
N_SSM_HEADS = D_INNER // SSM_HEAD_DIM
N_GROUPS = 4
D_STATE = 128
CONV_WIDTH = 4
BC_DIM = N_GROUPS * D_STATE
CONV_DIM = D_INNER + 2 * BC_DIM
N_EXPERTS = 32
TOP_K = 4
D_FF = D_MODEL
SWIGLU_ALPHA = 1.702
SWIGLU_LIMIT = 7.0
EPS = 1e-5
NEG = -1e30
LOG2E = math.log2(math.e)

LANES = 128
KV_DUP = 2 * KV_DIM
N_PAIRS = N_HEADS // 2
N_SSM_PAIRS = N_SSM_HEADS // 2
CTX = 3 * BLK
VMEM_LIMIT = 56 * 1024 * 1024

PROJ_SEGS = (("q", Q_DIM), ("k", KV_DUP), ("v", KV_DUP), ("z", D_INNER), ("xbc", CONV_DIM),
             ("gate", 2 * D_MODEL))
PROJ_N = sum(w for _, w in PROJ_SEGS)
PROJ_CHUNK = 512


def _dot(a, b):
    return jnp.dot(a, b, preferred_element_type=F32)


def _dot_nt(a, b):
    return lax.dot_general(a, b, (((1,), (1,)), ((), ())), preferred_element_type=F32)


def _sigmoid(x):
    return 0.5 * jnp.tanh(0.5 * x) + 0.5


def _silu(x):
    h = 0.5 * x
    return h * jnp.tanh(h) + h


def _const_spec(shape):
    nd = len(shape)
    return pl.BlockSpec(shape, lambda *_: (0,) * nd, pipeline_mode=pl.Buffered(1))


def _inproj_kernel(x_ref, g_ref, w_ref, wdt_ref, q_ref, k_ref, v_ref, z_ref, xbc_ref, gate_ref,
                   dt_ref):
    x = x_ref[...]
    ms = jnp.mean(x * x, axis=-1, keepdims=True)
    u = (x * lax.rsqrt(ms + EPS) * g_ref[...]).astype(BF16)
    outs = (q_ref, k_ref, v_ref, z_ref, xbc_ref, gate_ref)
    off = 0
    for ref, (_, width) in zip(outs, PROJ_SEGS):
        for c0 in range(0, width, PROJ_CHUNK):
            ref[:, c0:c0 + PROJ_CHUNK] = _dot(
                u, w_ref[:, off + c0:off + c0 + PROJ_CHUNK]).astype(BF16)
        off += width
    dt_ref[...] = _dot(u, wdt_ref[...])


def _inproj(x2d, g, w, wdt, tm):
    rows = x2d.shape[0]
    row = lambda i: (i, 0)
    out_shape = [jax.ShapeDtypeStruct((rows, wd), BF16) for _, wd in PROJ_SEGS]
    out_shape.append(jax.ShapeDtypeStruct((rows, LANES), F32))
    out_specs = [pl.BlockSpec((tm, wd), row) for _, wd in PROJ_SEGS]
    out_specs.append(pl.BlockSpec((tm, LANES), row))
    return pl.pallas_call(
        _inproj_kernel,
        grid=(rows // tm,),
        in_specs=[pl.BlockSpec((tm, D_MODEL), row), _const_spec((1, D_MODEL)),
                  _const_spec((D_MODEL, PROJ_N)), _const_spec((D_MODEL, LANES))],
        out_specs=out_specs,
        out_shape=out_shape,
        compiler_params=pltpu.CompilerParams(dimension_semantics=("parallel",),
                                             vmem_limit_bytes=VMEM_LIMIT),
        name="inproj",
    )(x2d, g, w, wdt)


def _attn_kernel(sink_ref, q_ref, kp_ref, kc_ref, km_ref, vp_ref, vc_ref, vm_ref, bwin_ref,
                 bmeta_ref, o_ref, s_ref, p_ref, inv_ref):
    lo = lax.broadcasted_iota(jnp.int32, (1, LANES), 1) < HEAD_DIM
    from_prev = (lax.broadcasted_iota(jnp.int32, (BLK, BLK), 1)
                 > lax.broadcasted_iota(jnp.int32, (BLK, BLK), 0))
    zero = jnp.zeros((), BF16)
    pairs_per_kv = Q_PER_KV // 2

    for g in range(N_KV_HEADS):
        cs = slice(g * LANES, (g + 1) * LANES)
        kk = jnp.concatenate([kp_ref[:, cs], kc_ref[:, cs], km_ref[:, cs]], axis=0)
        k_lo = jnp.where(lo, kk, zero)
        k_hi = jnp.where(lo, zero, kk)
        for j in range(pairs_per_kv):
            pi = g * pairs_per_kv + j
            qp = q_ref[:, pi * LANES:(pi + 1) * LANES]
            s_ref[2 * pi] = _dot_nt(qp, k_lo)
            s_ref[2 * pi + 1] = _dot_nt(qp, k_hi)

    for pi in range(N_PAIRS):
        invs = []
        for t in range(2):
            h = 2 * pi + t
            s_win = (jnp.where(from_prev, s_ref[h, :, :BLK], s_ref[h, :, BLK:2 * BLK])
                     + bwin_ref[0, h])
            s_meta = s_ref[h, :, 2 * BLK:] + bmeta_ref[0, h]
            sink = sink_ref[h]
            m = jnp.maximum(jnp.max(jnp.maximum(s_win, s_meta), axis=-1, keepdims=True), sink)
            p_win = jnp.exp2(s_win - m)
            p_meta = jnp.exp2(s_meta - m)
            den = jnp.sum(p_win + p_meta, axis=-1, keepdims=True) + jnp.exp2(sink - m)
            c0 = t * CTX
            p_ref[pi, :, c0:c0 + BLK] = jnp.where(from_prev, p_win, 0.0).astype(BF16)
            p_ref[pi, :, c0 + BLK:c0 + 2 * BLK] = jnp.where(from_prev, 0.0, p_win).astype(BF16)
            p_ref[pi, :, c0 + 2 * BLK:c0 + CTX] = p_meta.astype(BF16)
            invs.append(1.0 / den)
        inv_ref[pi] = jnp.where(lo, invs[0], invs[1])

    for g in range(N_KV_HEADS):
        cs = slice(g * LANES, (g + 1) * LANES)
        vv = jnp.concatenate([vp_ref[:, cs], vc_ref[:, cs], vm_ref[:, cs]], axis=0)
        v_bd = jnp.concatenate([jnp.where(lo, vv, zero), jnp.where(lo, zero, vv)], axis=0)
        for j in range(pairs_per_kv):
            pi = g * pairs_per_kv + j
            o = _dot(p_ref[pi], v_bd) * inv_ref[pi]
            o_ref[:, pi * LANES:(pi + 1) * LANES] = o.astype(BF16)


def _attention(q, k, v, k_meta, v_meta, bias_win, bias_meta, sinks, bsz, nb):
    cur = lambda b, n, *_: (b * nb + n, 0)
    prev = lambda b, n, *_: (jnp.maximum(b * nb + n - 1, 0), 0)
    const2 = lambda b, n, *_: (0, 0)
    grid_spec = pltpu.PrefetchScalarGridSpec(
        num_scalar_prefetch=1,
        grid=(bsz, nb),
        in_specs=[
            pl.BlockSpec((BLK, Q_DIM), cur),
            pl.BlockSpec((BLK, KV_DUP), prev), pl.BlockSpec((BLK, KV_DUP), cur),
            pl.BlockSpec((BLK, KV_DUP), const2),
            pl.BlockSpec((BLK, KV_DUP), prev), pl.BlockSpec((BLK, KV_DUP), cur),
            pl.BlockSpec((BLK, KV_DUP), const2),
            pl.BlockSpec((1, N_HEADS, BLK, BLK), lambda b, n, *_: (jnp.minimum(n, 1), 0, 0, 0)),
            pl.BlockSpec((1, N_HEADS, BLK, LANES), lambda b, n, *_: (n, 0, 0, 0)),
        ],
        out_specs=pl.BlockSpec((BLK, Q_DIM), cur),
        scratch_shapes=[pltpu.VMEM((N_HEADS, BLK, CTX), F32),
                        pltpu.VMEM((N_PAIRS, BLK, 2 * CTX), BF16),
                        pltpu.VMEM((N_PAIRS, BLK, LANES), F32)],
    )
    return pl.pallas_call(
        _attn_kernel,
        grid_spec=grid_spec,
        out_shape=jax.ShapeDtypeStruct((bsz * nb * BLK, Q_DIM), BF16),
        compiler_params=pltpu.CompilerParams(dimension_semantics=("parallel", "arbitrary"),
                                             vmem_limit_bytes=VMEM_LIMIT),
        name="swa_attention",
    )(sinks, q, k, k, k_meta, v, v, v_meta, bias_win, bias_meta)


CONV_COLS = 256
TAIL = 16


def _ssd_kernel(xbc_ref, z_ref, dtraw_ref, tail0_ref, h0_ref, convw_ref, convb_ref, dtb_ref,
                alog_ref, dskip_ref, ng_ref, y_ref, hfin_ref, win_ref, xc_ref, st_ref, *, n_pad):
    for s in range(xbc_ref.shape[0]):
        _ssd_chunk(xbc_ref.at[s], z_ref.at[s], dtraw_ref.at[s], tail0_ref, h0_ref, convw_ref,
                   convb_ref, dtb_ref, alog_ref, dskip_ref, ng_ref, y_ref.at[s], hfin_ref.at[s],
                   win_ref.at[s], xc_ref.at[s], st_ref.at[s], n_pad)


def _ssd_chunk(xbc_ref, z_ref, dtraw_ref, tail0_ref, h0_ref, convw_ref, convb_ref, dtb_ref,
               alog_ref, dskip_ref, ng_ref, y_ref, hfin_ref, win_ref, xc_ref, st_ref, n_pad):
    c = pl.program_id(1)

    @pl.when(c == 0)
    def _():
        win_ref[0:TAIL, :] = tail0_ref[...]
        st_ref[...] = h0_ref[...]

    win_ref[TAIL:TAIL + BLK, :] = xbc_ref[...]
    sr = lax.broadcasted_iota(jnp.int32, (BLK, TAIL + BLK), 0)
    sc = lax.broadcasted_iota(jnp.int32, (BLK, TAIL + BLK), 1)
    shifts = [jnp.where(sc == sr + (TAIL - (CONV_WIDTH - 1) + w), 1.0, 0.0).astype(BF16)
              for w in range(CONV_WIDTH)]
    live = lax.broadcasted_iota(jnp.int32, (BLK, 1), 0) >= n_pad
    for c0 in range(0, CONV_DIM, CONV_COLS):
        cs = slice(c0, c0 + CONV_COLS)
        wv = win_ref[:, cs]
        acc = convb_ref[:, cs]
        for w in range(CONV_WIDTH):
            acc = acc + convw_ref[w:w + 1, cs] * _dot(shifts[w], wv)
        act = _silu(acc)
        if n_pad:
            act = jnp.where(live, act, 0.0)
        xc_ref[:, cs] = act
    win_ref[0:TAIL, :] = win_ref[BLK:BLK + TAIL, :]

    lane_t = lax.broadcasted_iota(jnp.int32, (N_SSM_HEADS, BLK), 1)
    x_t = dtraw_ref[...].T[:N_SSM_HEADS] + dtb_ref[...]
    dt_t = jnp.maximum(x_t, 0.0) + jnp.log1p(jnp.exp(-jnp.abs(x_t)))
    if n_pad:
        dt_t = jnp.where(lane_t >= n_pad, dt_t, 0.0)
    a_cs_t = dt_t * (-jnp.exp(alog_ref[...]))
    shift = 1
    while shift < BLK:
        a_cs_t = a_cs_t + jnp.where(lane_t >= shift, pltpu.roll(a_cs_t, shift, 1), 0.0)
        shift *= 2
    w_t = dt_t * jnp.exp(a_cs_t[:, BLK - 1:BLK] - a_cs_t)
    src_t = a_cs_t - jnp.log(dt_t)
    fill = jnp.zeros((BLK - N_SSM_HEADS, BLK), F32)
    a_cs = jnp.concatenate([a_cs_t, fill], axis=0).T
    e_cs = jnp.exp(a_cs)
    ri = lax.broadcasted_iota(jnp.int32, (BLK, BLK), 0)
    ci = lax.broadcasted_iota(jnp.int32, (BLK, BLK), 1)
    causal = ri >= ci
    lo = lax.broadcasted_iota(jnp.int32, (1, LANES), 1) < SSM_HEAD_DIM

    pairs_per_group = N_SSM_PAIRS // N_GROUPS
    for g in range(N_GROUPS):
        b_g = xc_ref[:, D_INNER + g * D_STATE:D_INNER + (g + 1) * D_STATE]
        c_g = xc_ref[:, D_INNER + BC_DIM + g * D_STATE:D_INNER + BC_DIM + (g + 1) * D_STATE]
        cb = _dot_nt(c_g.astype(BF16), b_g.astype(BF16))
        b_gt = b_g.T
        ys = []
        for j in range(pairs_per_group):
            pi = g * pairs_per_group + j
            ps = slice(pi * LANES, (pi + 1) * LANES)
            xs = xc_ref[:, ps]
            xs_b = xs.astype(BF16)
            st = st_ref[pi]
            rhs = jnp.concatenate([xs_b, st.astype(BF16)], axis=0)
            y_h, st_h = [], []
            for t in range(2):
                h = 2 * pi + t
                diff = a_cs[:, h:h + 1] - src_t[h:h + 1, :]
                m_h = cb * jnp.exp(jnp.where(causal, diff, -jnp.inf))
                ce = c_g * e_cs[:, h:h + 1]
                lhs = jnp.concatenate([m_h, ce], axis=1).astype(BF16)
                y_h.append(_dot(lhs, rhs))
                s_new = _dot((b_gt * w_t[h:h + 1, :]).astype(BF16), xs_b)
                st_h.append(st * e_cs[BLK - 1:BLK, h:h + 1] + s_new)
            st_ref[pi] = jnp.where(lo, st_h[0], st_h[1])
            ys.append(jnp.where(lo, y_h[0], y_h[1]) + dskip_ref[:, ps] * xs)
        gs = slice(g * (D_INNER // N_GROUPS), (g + 1) * (D_INNER // N_GROUPS))
        zg = z_ref[:, gs].astype(F32)
        yg = jnp.concatenate(ys, axis=1) * _silu(zg)
        ms = jnp.mean(yg * yg, axis=-1, keepdims=True)
        y_ref[:, gs] = (yg * lax.rsqrt(ms + EPS) * ng_ref[:, gs]).astype(BF16)

    @pl.when(c == pl.num_programs(1) - 1)
    def _():
        hfin_ref[0] = st_ref[...]


def _ssd(xbc, z, dtraw, tail0, h0, convw, convb, dtb, alog, dskip, ng, bsz, nc, n_pad):
    n_seq = 2 if bsz % 2 == 0 else 1
    nbg = bsz // n_seq
    rows_g = nbg * nc * BLK
    cur = lambda b, c: (0, b * nc + c, 0)
    split = lambda a: a.reshape(n_seq, rows_g, a.shape[-1])
    y, h_fin = pl.pallas_call(
        functools.partial(_ssd_kernel, n_pad=n_pad),
        grid=(nbg, nc),
        in_specs=[
            pl.BlockSpec((n_seq, BLK, CONV_DIM), cur), pl.BlockSpec((n_seq, BLK, D_INNER), cur),
            pl.BlockSpec((n_seq, BLK, LANES), cur),
            _const_spec((TAIL, CONV_DIM)), _const_spec((N_SSM_PAIRS, D_STATE, LANES)),
            _const_spec((CONV_WIDTH, CONV_DIM)), _const_spec((1, CONV_DIM)),
            _const_spec((N_SSM_HEADS, BLK)), _const_spec((N_SSM_HEADS, BLK)),
            _const_spec((1, D_INNER)), _const_spec((1, D_INNER)),
        ],
        out_specs=[pl.BlockSpec((n_seq, BLK, D_INNER), cur),
                   pl.BlockSpec((n_seq, 1, N_SSM_PAIRS, D_STATE, LANES),
                                lambda b, c: (0, b, 0, 0, 0))],
        out_shape=[jax.ShapeDtypeStruct((n_seq, rows_g, D_INNER), BF16),
                   jax.ShapeDtypeStruct((n_seq, nbg, N_SSM_PAIRS, D_STATE, LANES), F32)],
        scratch_shapes=[pltpu.VMEM((n_seq, TAIL + BLK, CONV_DIM), BF16),
                        pltpu.VMEM((n_seq, BLK, CONV_DIM), F32),
                        pltpu.VMEM((n_seq, N_SSM_PAIRS, D_STATE, LANES), F32)],
        compiler_params=pltpu.CompilerParams(dimension_semantics=("parallel", "arbitrary"),
                                             vmem_limit_bytes=VMEM_LIMIT),
        name="ssd_mixer",
    )(split(xbc), split(z), split(dtraw), tail0, h0, convw, convb, dtb, alog, dskip, ng)
    return (y.reshape(bsz * nc * BLK, D_INNER),
            h_fin.reshape(bsz, N_SSM_PAIRS, D_STATE, LANES))


def _merge_kernel(ya_ref, ys_ref, gl_ref, x_ref, wa_ref, ws_ref, wo_ref, g_ref, wr_ref, br_ref,
                  h_ref, u_ref, idx_ref, tw_ref, cnt_ref):
    g0 = _sigmoid(gl_ref[:, :D_MODEL].astype(F32))
    g1 = _sigmoid(gl_ref[:, D_MODEL:].astype(F32))
    merged = g0 * _dot(ya_ref[...], wa_ref[...]) + g1 * _dot(ys_ref[...], ws_ref[...])
    h = x_ref[...] + _dot(merged.astype(BF16), wo_ref[...])
    h_ref[...] = h
    ms = jnp.mean(h * h, axis=-1, keepdims=True)
    u = (h * lax.rsqrt(ms + EPS) * g_ref[...]).astype(BF16)
    u_ref[...] = u
    logits = _dot(u, wr_ref[...]) + br_ref[...]
    lane = lax.broadcasted_iota(jnp.int32, logits.shape, 1).astype(F32)
    idx_out = jnp.zeros(logits.shape, F32)
    val_out = jnp.zeros(logits.shape, F32)
    chosen = jnp.zeros(logits.shape, F32)
    top = None
    den = None
    for k in range(TOP_K):
        m = jnp.max(logits, axis=-1, keepdims=True)
        sel = jnp.min(jnp.where(logits == m, lane, float(LANES)), axis=-1, keepdims=True)
        if k == 0:
            top = m
        e = jnp.exp(m - top)
        den = e if k == 0 else den + e
        idx_out = jnp.where(lane == k, sel, idx_out)
        val_out = jnp.where(lane == k, e, val_out)
        chosen = jnp.where(lane == sel, 1.0, chosen)
        logits = jnp.where(lane == sel, -jnp.inf, logits)
    idx_ref[...] = idx_out.astype(jnp.int32)
    tw_ref[...] = val_out / den

    @pl.when(pl.program_id(0) == 0)
    def _():
        cnt_ref[...] = jnp.zeros_like(cnt_ref)

    cnt_ref[...] += jnp.sum(chosen, axis=0, keepdims=True)


def _merge(ya, ys, gl, x2d, wa, ws, wo, g, wr, br, tm):
    rows = x2d.shape[0]
    row = lambda i: (i, 0)
    return pl.pallas_call(
        _merge_kernel,
        grid=(rows // tm,),
        in_specs=[pl.BlockSpec((tm, Q_DIM), row), pl.BlockSpec((tm, D_INNER), row),
                  pl.BlockSpec((tm, 2 * D_MODEL), row), pl.BlockSpec((tm, D_MODEL), row),
                  _const_spec((Q_DIM, D_MODEL)), _const_spec((D_INNER, D_MODEL)),
                  _const_spec((D_MODEL, D_MODEL)), _const_spec((1, D_MODEL)),
                  _const_spec((D_MODEL, LANES)), _const_spec((1, LANES))],
        out_specs=[pl.BlockSpec((tm, D_MODEL), row), pl.BlockSpec((tm, D_MODEL), row),
                   pl.BlockSpec((tm, LANES), row), pl.BlockSpec((tm, LANES), row),
                   pl.BlockSpec((8, LANES), lambda i: (0, 0))],
        out_shape=[jax.ShapeDtypeStruct((rows, D_MODEL), F32),
                   jax.ShapeDtypeStruct((rows, D_MODEL), BF16),
                   jax.ShapeDtypeStruct((rows, LANES), jnp.int32),
                   jax.ShapeDtypeStruct((rows, LANES), F32),
                   jax.ShapeDtypeStruct((8, LANES), F32)],
        compiler_params=pltpu.CompilerParams(dimension_semantics=("arbitrary",),
                                             vmem_limit_bytes=VMEM_LIMIT),
        name="merge_router",
    )(ya, ys, gl, x2d, wa, ws, wo, g, wr, br)


def _route_kernel(idx_ref, cnt_ref, dest_ref, info_ref, base_ref, *, tm_e):
    i = pl.program_id(0)
    tr = idx_ref.shape[0]
    lane = lax.broadcasted_iota(jnp.int32, (tr, LANES), 1)
    idx = idx_ref[...]
    onehot = [jnp.where(lane == idx[:, k:k + 1], 1.0, 0.0) for k in range(TOP_K)]
    oh_all = onehot[0] + onehot[1] + onehot[2] + onehot[3]

    @pl.when(i == 0)
    def _():
        counts = cnt_ref[...]
        padded = jnp.floor((counts + (tm_e - 1)) * (1.0 / tm_e)) * tm_e
        lane8 = lax.broadcasted_iota(jnp.int32, (8, LANES), 1)
        incl = padded
        shift = 1
        while shift < LANES:
            incl = incl + jnp.where(lane8 >= shift, pltpu.roll(incl, shift, 1), 0.0)
            shift *= 2
        pstart = incl - padded
        base_ref[...] = pstart[0:1]
        row8 = lax.broadcasted_iota(jnp.int32, (8, LANES), 0)
        info = jnp.where(row8 == 0, counts, jnp.where(row8 == 1, pstart, incl))
        info_ref[...] = info.astype(jnp.int32)

    r = lax.broadcasted_iota(jnp.int32, (tr, tr), 0)
    c = lax.broadcasted_iota(jnp.int32, (tr, tr), 1)
    before = jnp.where(r > c, 1.0, 0.0).astype(BF16)
    earlier = _dot(before, oh_all.astype(BF16))
    rank = base_ref[...] + earlier
    dest = jnp.zeros((tr, LANES), F32)
    for k in range(TOP_K):
        d_k = jnp.sum(onehot[k] * rank, axis=-1, keepdims=True)
        dest = jnp.where(lane == k, d_k, dest)
    dest_ref[...] = dest.T[0:8].astype(jnp.int32)
    base_ref[...] += jnp.sum(oh_all, axis=0, keepdims=True)


def _route(top_idx, counts, tm_e, tr):
    rows = top_idx.shape[0]
    return pl.pallas_call(
        functools.partial(_route_kernel, tm_e=tm_e),
        grid=(rows // tr,),
        in_specs=[pl.BlockSpec((tr, LANES), lambda i: (i, 0)), _const_spec((8, LANES))],
        out_specs=[pl.BlockSpec((8, tr), lambda i: (0, i)),
                   pl.BlockSpec((8, LANES), lambda i: (0, 0))],
        out_shape=[jax.ShapeDtypeStruct((8, rows), jnp.int32),
                   jax.ShapeDtypeStruct((8, LANES), jnp.int32)],
        scratch_shapes=[pltpu.VMEM((1, LANES), F32)],
        compiler_params=pltpu.CompilerParams(dimension_semantics=("arbitrary",),
                                             vmem_limit_bytes=VMEM_LIMIT),
        name="route",
    )(top_idx, counts)


FF_CHUNK = 2 * LANES


def _moe_kernel(te_ref, nv_ref, x_ref, w1_ref, b1_ref, w2_ref, b2_ref, y_ref, w1s_ref, w2s_ref):
    s = pl.program_id(0)
    n = pl.num_programs(0) - 1
    tile = s - 1
    valid = jnp.logical_and(s >= 1, tile < nv_ref[0])
    prev = te_ref[jnp.maximum(tile - 1, 0)]
    cur = te_ref[jnp.maximum(tile, 0)]
    nxt = te_ref[jnp.minimum(s, n - 1)]

    @pl.when(jnp.logical_and(valid, jnp.logical_or(tile == 0, cur != prev)))
    def _():
        w2s_ref[...] = w2_ref[0].astype(BF16)

    @pl.when(valid)
    def _():
        x = x_ref[...]
        acts = []
        for c0 in range(0, 2 * D_FF, FF_CHUNK):
            cs = slice(c0, c0 + FF_CHUNK)
            hh = _dot(x, w1s_ref[:, cs]) + b1_ref[0, :, cs]
            glu = jnp.minimum(hh[:, :LANES], SWIGLU_LIMIT)
            lin = jnp.clip(hh[:, LANES:], -SWIGLU_LIMIT, SWIGLU_LIMIT)
            acts.append((glu * _sigmoid(SWIGLU_ALPHA * glu) * (lin + 1.0)).astype(BF16))
        act = jnp.concatenate(acts, axis=1)
        y_ref[...] = (_dot(act, w2s_ref[...]) + b2_ref[0]).astype(BF16)

    @pl.when(jnp.logical_and(s >= 1, jnp.logical_not(valid)))
    def _():
        y_ref[...] = jnp.zeros_like(y_ref)

    next_valid = jnp.logical_and(s < n, s < nv_ref[0])
    @pl.when(jnp.logical_and(next_valid, jnp.logical_or(s == 0, nxt != cur)))
    def _():
        r = lax.broadcasted_iota(jnp.int32, (FF_CHUNK, FF_CHUNK), 0)
        c = lax.broadcasted_iota(jnp.int32, (FF_CHUNK, FF_CHUNK), 1)
        src = jnp.where(c < LANES, 2 * c, 2 * (c - LANES) + 1)
        perm = jnp.where(r == src, 1.0, 0.0).astype(BF16)
        for c0 in range(0, 2 * D_FF, FF_CHUNK):
            cs = slice(c0, c0 + FF_CHUNK)
            w1s_ref[:, cs] = _dot(w1_ref[0, :, cs].astype(BF16), perm).astype(BF16)


def _moe_kernel_into(te_ref, nv_ref, x_ref, w1_ref, b1_ref, w2_ref, b2_ref, yprev_ref, y_ref,
                     w1s_ref, w2s_ref):
    del yprev_ref
    _moe_kernel(te_ref, nv_ref, x_ref, w1_ref, b1_ref, w2_ref, b2_ref, y_ref, w1s_ref, w2s_ref)


def _moe(tile_e, n_valid, xrows, w1, b1, w2, b2, tm, tile0, n_tiles_total, y_prev=None):
    n_tiles = xrows.shape[0] // tm
    tile = lambda s: jnp.maximum(s - 1, 0)
    wsel = lambda s, te, nv: (te[tile(s)], 0, 0)
    wsel_ahead = lambda s, te, nv: (te[jnp.minimum(s, n_tiles - 1)], 0, 0)
    in_specs = [pl.BlockSpec((tm, D_MODEL), lambda s, te, nv: (tile(s), 0)),
                pl.BlockSpec((1, D_MODEL, 2 * D_FF), wsel_ahead),
                pl.BlockSpec((1, 1, 2 * D_FF), wsel),
                pl.BlockSpec((1, D_FF, D_MODEL), wsel), pl.BlockSpec((1, 1, D_MODEL), wsel)]
    args = [tile_e, n_valid, xrows, w1, b1, w2, b2]
    if y_prev is not None:
        in_specs.append(pl.BlockSpec(memory_space=pl.ANY))
        args.append(y_prev)
    grid_spec = pltpu.PrefetchScalarGridSpec(
        num_scalar_prefetch=2,
        grid=(n_tiles + 1,),
        in_specs=in_specs,
        out_specs=pl.BlockSpec((tm, D_MODEL), lambda s, te, nv: (tile(s) + tile0, 0)),
        scratch_shapes=[pltpu.VMEM((D_MODEL, 2 * D_FF), BF16), pltpu.VMEM((D_FF, D_MODEL), BF16)],
    )
    return pl.pallas_call(
        _moe_kernel if y_prev is None else _moe_kernel_into,
        grid_spec=grid_spec,
        out_shape=jax.ShapeDtypeStruct((n_tiles_total * tm, D_MODEL), BF16),
        input_output_aliases={} if y_prev is None else {len(args) - 1: 0},
        compiler_params=pltpu.CompilerParams(dimension_semantics=("arbitrary",),
                                             vmem_limit_bytes=VMEM_LIMIT),
        name="moe_experts",
    )(*args)


def _final_kernel(h_ref, y0_ref, y1_ref, y2_ref, y3_ref, tw_ref, g_ref, o_ref):
    h = h_ref[...]
    tw = tw_ref[...]
    for k, y_ref in enumerate((y0_ref, y1_ref, y2_ref, y3_ref)):
        h = h + tw[:, k:k + 1] * y_ref[...].astype(F32)
    ms = jnp.mean(h * h, axis=-1, keepdims=True)
    o_ref[...] = h * lax.rsqrt(ms + EPS) * g_ref[...]


def _final(h, yg, tw, g, tm):
    rows = h.shape[0]
    row = lambda i: (i, 0)
    return pl.pallas_call(
        _final_kernel,
        grid=(rows // tm,),
        in_specs=[pl.BlockSpec((tm, D_MODEL), row)] * (1 + TOP_K)
        + [pl.BlockSpec((tm, LANES), row), _const_spec((1, D_MODEL))],
        out_specs=pl.BlockSpec((tm, D_MODEL), row),
        out_shape=jax.ShapeDtypeStruct((rows, D_MODEL), F32),
        compiler_params=pltpu.CompilerParams(dimension_semantics=("parallel",),
                                             vmem_limit_bytes=VMEM_LIMIT),
        name="combine_final_norm",
    )(h, *yg, tw, g)


def _t5_bucket(dist):
    n = jnp.maximum(dist, 0)
    max_exact = N_BUCKETS // 2
    nf = jnp.maximum(n, 1).astype(F32)
    large = max_exact + (jnp.log(nf / max_exact) / math.log(MAX_DISTANCE / max_exact)
                         * (N_BUCKETS - max_exact)).astype(jnp.int32)
    large = jnp.minimum(large, N_BUCKETS - 1)
    return jnp.where(n < max_exact, n, large)


def _bias_tables(rel_bias, nb):
    table = rel_bias.astype(F32) * LOG2E
    j = jnp.arange(BLK, dtype=jnp.int32)[:, None]
    s = jnp.arange(BLK, dtype=jnp.int32)[None, :]
    from_prev = s > j
    dist = jnp.where(from_prev, j + BLK - s, j - s)

    def lookup(bucket):
        onehot = (bucket[..., None] == jnp.arange(N_BUCKETS, dtype=jnp.int32)).astype(F32)
        return jnp.einsum('...b,bh->...h', onehot, table, precision=lax.Precision.HIGHEST)

    win = lookup(_t5_bucket(dist))
    win = jnp.stack([jnp.where(from_prev[:, :, None], NEG, win), win])
    n_idx = jnp.arange(nb, dtype=jnp.int32)[:, None, None]
    m = jnp.arange(N_META, dtype=jnp.int32)[None, None, :]
    dist_m = N_META + n_idx * BLK + j[None] - m
    meta = lookup(_t5_bucket(dist_m))
    meta = jnp.pad(meta, ((0, 0), (0, 0), (0, LANES - N_META), (0, 0)), constant_values=NEG)
    return win.transpose(0, 3, 1, 2), meta.transpose(0, 3, 1, 2)


def _dup_heads(w):
    w = w.reshape(w.shape[0], N_KV_HEADS, 1, HEAD_DIM)
    return jnp.broadcast_to(w, (w.shape[0], N_KV_HEADS, 2, HEAD_DIM)).reshape(w.shape[0], KV_DUP)


def _pad_lanes(v, value=0.0):
    v = v.reshape(1, -1).astype(F32)
    return jnp.pad(v, ((0, 0), (0, LANES - v.shape[1])), constant_values=value)


def _head_rows(v):
    return jnp.broadcast_to(v.astype(F32)[:, None], (N_SSM_HEADS, BLK))


def _row_tile(rows, target):
    tm = min(rows, target)
    assert rows % tm == 0
    return tm


def kernel(x, meta_tokens, rel_bias, norm_mix_g, w_in, attn_sinks, conv_w, conv_b, dt_bias, a_log,
           d_skip, ssm_norm_g, w_attn_br, w_ssm_br, w_out, norm_ffn_g, w_router, b_router,
           w_exp_in, b_exp_in, w_exp_out, b_exp_out, norm_final_g):
    bsz, seq, _ = x.shape
    assert seq % BLK == 0 and w_in.shape[0] == 1
    nb = seq // BLK
    rows = bsz * seq
    x2d = x.reshape(rows, D_MODEL)

    wi = w_in[0]
    o_k, o_v, o_z = Q_DIM, Q_DIM + KV_DIM, Q_DIM + 2 * KV_DIM
    o_x, o_dt = o_z + D_INNER, o_z + D_INNER + CONV_DIM
    o_g = o_dt + N_SSM_HEADS
    w_proj = jnp.concatenate([
        wi[:, :o_k] * (HEAD_DIM ** -0.5 * LOG2E), _dup_heads(wi[:, o_k:o_v]),
        _dup_heads(wi[:, o_v:o_z]),
        wi[:, o_z:o_x], wi[:, o_x:o_dt], wi[:, o_g:]], axis=1).astype(BF16)
    w_dt = jnp.pad(wi[:, o_dt:o_g], ((0, 0), (0, LANES - N_SSM_HEADS))).astype(BF16)
    g_mix = norm_mix_g[0].reshape(1, D_MODEL)

    tm = _row_tile(rows, 512)
    q, k, v, z, xbc, gate, dtraw = _inproj(x2d, g_mix, w_proj, w_dt, tm)
    _, k_m, v_m, z_m, xbc_m, _, dtraw_m = _inproj(meta_tokens.astype(F32), g_mix, w_proj, w_dt,
                                                  N_META)

    pad_meta = ((BLK - N_META, 0), (0, 0))
    pad_meta_tail = ((0, BLK - N_META), (0, 0))
    bias_win, bias_meta = _bias_tables(rel_bias, nb)
    y_attn = _attention(q, k, v, jnp.pad(k_m, pad_meta_tail), jnp.pad(v_m, pad_meta_tail),
                        bias_win, bias_meta, attn_sinks[0].astype(F32) * LOG2E, bsz, nb)

    ssd_params = (conv_w[0].astype(F32), conv_b[0].reshape(1, CONV_DIM).astype(F32),
                  _head_rows(dt_bias[0]), _head_rows(a_log[0]),
                  jnp.repeat(d_skip[0].astype(F32), SSM_HEAD_DIM).reshape(1, D_INNER),
                  ssm_norm_g[0].reshape(1, D_INNER).astype(F32))
    xbc_mp = jnp.pad(xbc_m, pad_meta)
    zero_state = jnp.zeros((N_SSM_PAIRS, D_STATE, LANES), F32)
    _, h_meta = _ssd(xbc_mp, jnp.pad(z_m, pad_meta), jnp.pad(dtraw_m, pad_meta),
                     jnp.zeros((TAIL, CONV_DIM), BF16), zero_state, *ssd_params, 1, 1,
                     BLK - N_META)
    y_ssm, _ = _ssd(xbc, z, dtraw, xbc_mp[BLK - TAIL:], h_meta[0], *ssd_params, bsz, nb, 0)

    w_r = jnp.pad(w_router[0], ((0, 0), (0, LANES - N_EXPERTS))).astype(BF16)
    b_r = _pad_lanes(b_router[0], NEG)
    h1, u2, top_idx, top_w, counts8 = _merge(
        y_attn, y_ssm, gate, x2d, w_attn_br[0].astype(BF16), w_ssm_br[0].astype(BF16),
        w_out[0].astype(BF16), norm_ffn_g[0].reshape(1, D_MODEL), w_r, b_r, tm)

    tm_e = 512
    n_tiles = -(-(rows * TOP_K + N_EXPERTS * (tm_e - 1)) // tm_e)
    dest_t, info = _route(top_idx, counts8, tm_e, _row_tile(rows, 1024))
    counts, pstart, pend = info[0, :N_EXPERTS], info[1, :N_EXPERTS], info[2, :N_EXPERTS]
    tile_start = jnp.arange(n_tiles, dtype=jnp.int32) * tm_e
    tile_e = jnp.minimum(jnp.sum(tile_start[:, None] >= pend[None, :], axis=1),
                         N_EXPERTS - 1).astype(jnp.int32)
    n_valid = pend[-1:] // tm_e
    n_rows = n_tiles * tm_e
    n_assign = rows * TOP_K
    slot = jnp.arange(tm_e, dtype=jnp.int32)[None, :]
    pad_key = (pstart + counts)[:, None] + slot
    pad_key = jnp.where(pad_key < pend[:, None], pad_key, n_rows).reshape(-1)
    tail_key = pend[-1] + jnp.arange(n_rows - n_assign, dtype=jnp.int32)
    tail_key = jnp.where(tail_key < n_rows, tail_key, n_rows)
    n_fill = pad_key.shape[0] + tail_key.shape[0]
    keys = jnp.concatenate([dest_t[:TOP_K].reshape(-1), pad_key, tail_key])
    toks = jnp.concatenate([jnp.tile(jnp.arange(rows, dtype=jnp.int32), TOP_K),
                            jnp.arange(n_fill, dtype=jnp.int32) % rows])
    row_tok = lax.sort((keys, toks), num_keys=1)[1][:n_rows]

    b1 = b_exp_in[0].astype(F32).reshape(N_EXPERTS, 2 * D_FF // FF_CHUNK, LANES, 2)
    b1 = b1.transpose(0, 1, 3, 2).reshape(N_EXPERTS, 1, 2 * D_FF)
    b2 = b_exp_out[0].reshape(N_EXPERTS, 1, D_MODEL).astype(F32)
    y_rows = None
    n_parts = 4
    bounds = [n_tiles * p // n_parts for p in range(n_parts + 1)]
    for t0, t1 in zip(bounds[:-1], bounds[1:]):
        y_rows = _moe(tile_e[t0:t1], jnp.clip(n_valid - t0, 0, t1 - t0),
                      u2[row_tok[t0 * tm_e:t1 * tm_e]], w_exp_in[0], b1, w_exp_out[0], b2, tm_e,
                      t0, n_tiles, y_rows)

    yg = [y_rows[dest_t[k]] for k in range(TOP_K)]
    out = _final(h1, yg, top_w, norm_final_g.reshape(1, D_MODEL).astype(F32), tm)
    return out.reshape(bsz, seq, D_MODEL)
```

```python
import functools
import math

import jax
import jax.numpy as jnp
from jax import lax
from jax.experimental import pallas as pl
from jax.experimental.pallas import tpu as pltpu

F32 = jnp.float32
BF16 = jnp.bfloat16

D_MODEL = 1024
N_META = 16
N_HEADS = 16
N_KV_HEADS = 4
HEAD_DIM = 64
Q_PER_KV = N_HEADS // N_KV_HEADS
Q_DIM = N_HEADS * HEAD_DIM
KV_DIM = N_KV_HEADS * HEAD_DIM
WINDOW = 128
BLK = 128
N_BUCKETS = 32
MAX_DISTANCE = 128
D_INNER = 2 * D_MODEL
SSM_HEAD_DIM = 64
N_SSM_HEADS = D_INNER // SSM_HEAD_DIM
N_GROUPS = 4
D_STATE = 128
CONV_WIDTH = 4
BC_DIM = N_GROUPS * D_STATE
CONV_DIM = D_INNER + 2 * BC_DIM
N_EXPERTS = 32
TOP_K = 4
D_FF = D_MODEL
SWIGLU_ALPHA = 1.702
SWIGLU_LIMIT = 7.0
EPS = 1e-5
NEG = -1e30
LOG2E = math.log2(math.e)

LANES = 128
KV_DUP = 2 * KV_DIM
N_PAIRS = N_HEADS // 2
N_SSM_PAIRS = N_SSM_HEADS // 2
CTX = 3 * BLK
VMEM_LIMIT = 56 * 1024 * 1024
ROW_TILE = 512
ROUTE_TILE = 1024
EXPERT_TILE = 512
MOE_CALLS = 4

PROJ_SEGS = (("q", Q_DIM), ("k", KV_DUP), ("v", KV_DUP), ("z", D_INNER), ("xbc", CONV_DIM),
             ("gate", 2 * D_MODEL))
PROJ_N = sum(w for _, w in PROJ_SEGS)
PROJ_CHUNK = 512


def _dot(a, b):
    return jnp.dot(a, b, preferred_element_type=F32)


def _dot_nt(a, b):
    return lax.dot_general(a, b, (((1,), (1,)), ((), ())), preferred_element_type=F32)


def _sigmoid(x):
    return 0.5 * jnp.tanh(0.5 * x) + 0.5


def _silu(x):
    h = 0.5 * x
    return h * jnp.tanh(h) + h


def _const_spec(shape):
    nd = len(shape)
    return pl.BlockSpec(shape, lambda *_: (0,) * nd, pipeline_mode=pl.Buffered(1))


def _inproj_kernel(x_ref, g_ref, w_ref, wdt_ref, q_ref, k_ref, v_ref, z_ref, xbc_ref, gate_ref,
                   dt_ref):
    x = x_ref[...]
    ms = jnp.mean(x * x, axis=-1, keepdims=True)
    u = (x * lax.rsqrt(ms + EPS) * g_ref[...]).astype(BF16)
    outs = (q_ref, k_ref, v_ref, z_ref, xbc_ref, gate_ref)
    off = 0
    for ref, (_, width) in zip(outs, PROJ_SEGS):
        for c0 in range(0, width, PROJ_CHUNK):
            ref[:, c0:c0 + PROJ_CHUNK] = _dot(
                u, w_ref[:, off + c0:off + c0 + PROJ_CHUNK]).astype(BF16)
        off += width
    dt_ref[...] = _dot(u, wdt_ref[...])


def _inproj(x2d, g, w, wdt, tm):
    rows = x2d.shape[0]
    row = lambda i: (i, 0)
    out_shape = [jax.ShapeDtypeStruct((rows, wd), BF16) for _, wd in PROJ_SEGS]
    out_shape.append(jax.ShapeDtypeStruct((rows, LANES), F32))
    out_specs = [pl.BlockSpec((tm, wd), row) for _, wd in PROJ_SEGS]
    out_specs.append(pl.BlockSpec((tm, LANES), row))
    return pl.pallas_call(
        _inproj_kernel,
        grid=(rows // tm,),
        in_specs=[pl.BlockSpec((tm, D_MODEL), row), _const_spec((1, D_MODEL)),
                  _const_spec((D_MODEL, PROJ_N)), _const_spec((D_MODEL, LANES))],
        out_specs=out_specs,
        out_shape=out_shape,
        compiler_params=pltpu.CompilerParams(dimension_semantics=("parallel",),
                                             vmem_limit_bytes=VMEM_LIMIT),
        name="inproj",
    )(x2d, g, w, wdt)


def _attn_kernel(sink_ref, q_ref, kp_ref, kc_ref, km_ref, vp_ref, vc_ref, vm_ref, bwin_ref,
                 bmeta_ref, o_ref, s_ref, p_ref, inv_ref):
    lo = lax.broadcasted_iota(jnp.int32, (1, LANES), 1) < HEAD_DIM
    from_prev = (lax.broadcasted_iota(jnp.int32, (BLK, BLK), 1)
                 > lax.broadcasted_iota(jnp.int32, (BLK, BLK), 0))
    zero = jnp.zeros((), BF16)
    pairs_per_kv = Q_PER_KV // 2

    for g in range(N_KV_HEADS):
        cs = slice(g * LANES, (g + 1) * LANES)
        kk = jnp.concatenate([kp_ref[:, cs], kc_ref[:, cs], km_ref[:, cs]], axis=0)
        k_lo = jnp.where(lo, kk, zero)
        k_hi = jnp.where(lo, zero, kk)
        for j in range(pairs_per_kv):
            pi = g * pairs_per_kv + j
            qp = q_ref[:, pi * LANES:(pi + 1) * LANES]
            s_ref[2 * pi] = _dot_nt(qp, k_lo)
            s_ref[2 * pi + 1] = _dot_nt(qp, k_hi)

    for pi in range(N_PAIRS):
        invs = []
        for t in range(2):
            h = 2 * pi + t
            s_win = (jnp.where(from_prev, s_ref[h, :, :BLK], s_ref[h, :, BLK:2 * BLK])
                     + bwin_ref[0, h])
            s_meta = s_ref[h, :, 2 * BLK:] + bmeta_ref[0, h]
            sink = sink_ref[h]
            m = jnp.maximum(jnp.max(jnp.maximum(s_win, s_meta), axis=-1, keepdims=True), sink)
            p_win = jnp.exp2(s_win - m)
            p_meta = jnp.exp2(s_meta - m)
            den = jnp.sum(p_win + p_meta, axis=-1, keepdims=True) + jnp.exp2(sink - m)
            c0 = t * CTX
            p_ref[pi, :, c0:c0 + BLK] = jnp.where(from_prev, p_win, 0.0).astype(BF16)
            p_ref[pi, :, c0 + BLK:c0 + 2 * BLK] = jnp.where(from_prev, 0.0, p_win).astype(BF16)
            p_ref[pi, :, c0 + 2 * BLK:c0 + CTX] = p_meta.astype(BF16)
            invs.append(1.0 / den)
        inv_ref[pi] = jnp.where(lo, invs[0], invs[1])

    for g in range(N_KV_HEADS):
        cs = slice(g * LANES, (g + 1) * LANES)
        vv = jnp.concatenate([vp_ref[:, cs], vc_ref[:, cs], vm_ref[:, cs]], axis=0)
        v_bd = jnp.concatenate([jnp.where(lo, vv, zero), jnp.where(lo, zero, vv)], axis=0)
        for j in range(pairs_per_kv):
            pi = g * pairs_per_kv + j
            o = _dot(p_ref[pi], v_bd) * inv_ref[pi]
            o_ref[:, pi * LANES:(pi + 1) * LANES] = o.astype(BF16)


def _attention(q, k, v, k_meta, v_meta, bias_win, bias_meta, sinks, bsz, nb):
    cur = lambda b, n, *_: (b * nb + n, 0)
    prev = lambda b, n, *_: (jnp.maximum(b * nb + n - 1, 0), 0)
    const2 = lambda b, n, *_: (0, 0)
    grid_spec = pltpu.PrefetchScalarGridSpec(
        num_scalar_prefetch=1,
        grid=(bsz, nb),
        in_specs=[
            pl.BlockSpec((BLK, Q_DIM), cur),
            pl.BlockSpec((BLK, KV_DUP), prev), pl.BlockSpec((BLK, KV_DUP), cur),
            pl.BlockSpec((BLK, KV_DUP), const2),
            pl.BlockSpec((BLK, KV_DUP), prev), pl.BlockSpec((BLK, KV_DUP), cur),
            pl.BlockSpec((BLK, KV_DUP), const2),
            pl.BlockSpec((1, N_HEADS, BLK, BLK), lambda b, n, *_: (jnp.minimum(n, 1), 0, 0, 0)),
            pl.BlockSpec((1, N_HEADS, BLK, LANES), lambda b, n, *_: (n, 0, 0, 0)),
        ],
        out_specs=pl.BlockSpec((BLK, Q_DIM), cur),
        scratch_shapes=[pltpu.VMEM((N_HEADS, BLK, CTX), F32),
                        pltpu.VMEM((N_PAIRS, BLK, 2 * CTX), BF16),
                        pltpu.VMEM((N_PAIRS, BLK, LANES), F32)],
    )
    return pl.pallas_call(
        _attn_kernel,
        grid_spec=grid_spec,
        out_shape=jax.ShapeDtypeStruct((bsz * nb * BLK, Q_DIM), BF16),
        compiler_params=pltpu.CompilerParams(dimension_semantics=("parallel", "arbitrary"),
                                             vmem_limit_bytes=VMEM_LIMIT),
        name="swa_attention",
    )(sinks, q, k, k, k_meta, v, v, v_meta, bias_win, bias_meta)


CONV_COLS = 256
TAIL = 16


def _ssd_kernel(xbc_ref, z_ref, dtraw_ref, tail0_ref, h0_ref, convw_ref, convb_ref, dtb_ref,
                alog_ref, dskip_ref, ng_ref, y_ref, hfin_ref, win_ref, xc_ref, st_ref, *, n_pad):
    for s in range(xbc_ref.shape[0]):
        _ssd_chunk(xbc_ref.at[s], z_ref.at[s], dtraw_ref.at[s], tail0_ref, h0_ref, convw_ref,
                   convb_ref, dtb_ref, alog_ref, dskip_ref, ng_ref, y_ref.at[s], hfin_ref.at[s],
                   win_ref.at[s], xc_ref.at[s], st_ref.at[s], n_pad)


def _ssd_chunk(xbc_ref, z_ref, dtraw_ref, tail0_ref, h0_ref, convw_ref, convb_ref, dtb_ref,
               alog_ref, dskip_ref, ng_ref, y_ref, hfin_ref, win_ref, xc_ref, st_ref, n_pad):
    c = pl.program_id(1)

    @pl.when(c == 0)
    def _():
        win_ref[0:TAIL, :] = tail0_ref[...]
        st_ref[...] = h0_ref[...]

    win_ref[TAIL:TAIL + BLK, :] = xbc_ref[...]
    sr = lax.broadcasted_iota(jnp.int32, (BLK, TAIL + BLK), 0)
    sc = lax.broadcasted_iota(jnp.int32, (BLK, TAIL + BLK), 1)
    shifts = [jnp.where(sc == sr + (TAIL - (CONV_WIDTH - 1) + w), 1.0, 0.0).astype(BF16)
              for w in range(CONV_WIDTH)]
    live = lax.broadcasted_iota(jnp.int32, (BLK, 1), 0) >= n_pad
    for c0 in range(0, CONV_DIM, CONV_COLS):
        cs = slice(c0, c0 + CONV_COLS)
        wv = win_ref[:, cs]
        acc = convb_ref[:, cs]
        for w in range(CONV_WIDTH):
            acc = acc + convw_ref[w:w + 1, cs] * _dot(shifts[w], wv)
        act = _silu(acc)
        if n_pad:
            act = jnp.where(live, act, 0.0)
        xc_ref[:, cs] = act
    win_ref[0:TAIL, :] = win_ref[BLK:BLK + TAIL, :]

    lane_t = lax.broadcasted_iota(jnp.int32, (N_SSM_HEADS, BLK), 1)
    x_t = dtraw_ref[...].T[:N_SSM_HEADS] + dtb_ref[...]
    dt_t = jnp.maximum(x_t, 0.0) + jnp.log1p(jnp.exp(-jnp.abs(x_t)))
    if n_pad:
        dt_t = jnp.where(lane_t >= n_pad, dt_t, 0.0)
    a_cs_t = dt_t * (-jnp.exp(alog_ref[...]))
    shift = 1
    while shift < BLK:
        a_cs_t = a_cs_t + jnp.where(lane_t >= shift, pltpu.roll(a_cs_t, shift, 1), 0.0)
        shift *= 2
    w_t = dt_t * jnp.exp(a_cs_t[:, BLK - 1:BLK] - a_cs_t)
    src_t = a_cs_t - jnp.log(dt_t)
    fill = jnp.zeros((BLK - N_SSM_HEADS, BLK), F32)
    a_cs = jnp.concatenate([a_cs_t, fill], axis=0).T
    e_cs = jnp.exp(a_cs)
    ri = lax.broadcasted_iota(jnp.int32, (BLK, BLK), 0)
    ci = lax.broadcasted_iota(jnp.int32, (BLK, BLK), 1)
    causal = ri >= ci
    lo = lax.broadcasted_iota(jnp.int32, (1, LANES), 1) < SSM_HEAD_DIM

    pairs_per_group = N_SSM_PAIRS // N_GROUPS
    for g in range(N_GROUPS):
        b_g = xc_ref[:, D_INNER + g * D_STATE:D_INNER + (g + 1) * D_STATE]
        c_g = xc_ref[:, D_INNER + BC_DIM + g * D_STATE:D_INNER + BC_DIM + (g + 1) * D_STATE]
        cb = _dot_nt(c_g.astype(BF16), b_g.astype(BF16))
        b_gt = b_g.T
        ys = []
        for j in range(pairs_per_group):
            pi = g * pairs_per_group + j
            ps = slice(pi * LANES, (pi + 1) * LANES)
            xs = xc_ref[:, ps]
            xs_b = xs.astype(BF16)
            st = st_ref[pi]
            rhs = jnp.concatenate([xs_b, st.astype(BF16)], axis=0)
            y_h, st_h = [], []
            for t in range(2):
                h = 2 * pi + t
                diff = a_cs[:, h:h + 1] - src_t[h:h + 1, :]
                m_h = cb * jnp.exp(jnp.where(causal, diff, -jnp.inf))
                ce = c_g * e_cs[:, h:h + 1]
                lhs = jnp.concatenate([m_h, ce], axis=1).astype(BF16)
                y_h.append(_dot(lhs, rhs))
                s_new = _dot((b_gt * w_t[h:h + 1, :]).astype(BF16), xs_b)
                st_h.append(st * e_cs[BLK - 1:BLK, h:h + 1] + s_new)
            st_ref[pi] = jnp.where(lo, st_h[0], st_h[1])
            ys.append(jnp.where(lo, y_h[0], y_h[1]) + dskip_ref[:, ps] * xs)
        gs = slice(g * (D_INNER // N_GROUPS), (g + 1) * (D_INNER // N_GROUPS))
        zg = z_ref[:, gs].astype(F32)
        yg = jnp.concatenate(ys, axis=1) * _silu(zg)
        ms = jnp.mean(yg * yg, axis=-1, keepdims=True)
        y_ref[:, gs] = (yg * lax.rsqrt(ms + EPS) * ng_ref[:, gs]).astype(BF16)

    @pl.when(c == pl.num_programs(1) - 1)
    def _():
        hfin_ref[0] = st_ref[...]


def _ssd(xbc, z, dtraw, tail0, h0, convw, convb, dtb, alog, dskip, ng, bsz, nc, n_pad):
    n_seq = 2 if bsz % 2 == 0 else 1
    nbg = bsz // n_seq
    rows_g = nbg * nc * BLK
    cur = lambda b, c: (0, b * nc + c, 0)
    split = lambda a: a.reshape(n_seq, rows_g, a.shape[-1])
    y, h_fin = pl.pallas_call(
        functools.partial(_ssd_kernel, n_pad=n_pad),
        grid=(nbg, nc),
        in_specs=[
            pl.BlockSpec((n_seq, BLK, CONV_DIM), cur), pl.BlockSpec((n_seq, BLK, D_INNER), cur),
            pl.BlockSpec((n_seq, BLK, LANES), cur),
            _const_spec((TAIL, CONV_DIM)), _const_spec((N_SSM_PAIRS, D_STATE, LANES)),
            _const_spec((CONV_WIDTH, CONV_DIM)), _const_spec((1, CONV_DIM)),
            _const_spec((N_SSM_HEADS, BLK)), _const_spec((N_SSM_HEADS, BLK)),
            _const_spec((1, D_INNER)), _const_spec((1, D_INNER)),
        ],
        out_specs=[pl.BlockSpec((n_seq, BLK, D_INNER), cur),
                   pl.BlockSpec((n_seq, 1, N_SSM_PAIRS, D_STATE, LANES),
                                lambda b, c: (0, b, 0, 0, 0))],
        out_shape=[jax.ShapeDtypeStruct((n_seq, rows_g, D_INNER), BF16),
                   jax.ShapeDtypeStruct((n_seq, nbg, N_SSM_PAIRS, D_STATE, LANES), F32)],
        scratch_shapes=[pltpu.VMEM((n_seq, TAIL + BLK, CONV_DIM), BF16),
                        pltpu.VMEM((n_seq, BLK, CONV_DIM), F32),
                        pltpu.VMEM((n_seq, N_SSM_PAIRS, D_STATE, LANES), F32)],
        compiler_params=pltpu.CompilerParams(dimension_semantics=("parallel", "arbitrary"),
                                             vmem_limit_bytes=VMEM_LIMIT),
        name="ssd_mixer",
    )(split(xbc), split(z), split(dtraw), tail0, h0, convw, convb, dtb, alog, dskip, ng)
    return (y.reshape(bsz * nc * BLK, D_INNER),
            h_fin.reshape(bsz, N_SSM_PAIRS, D_STATE, LANES))


def _merge_kernel(ya_ref, ys_ref, gl_ref, x_ref, wa_ref, ws_ref, wo_ref, g_ref, wr_ref, br_ref,
                  h_ref, u_ref, idx_ref, tw_ref, cnt_ref):
    g0 = _sigmoid(gl_ref[:, :D_MODEL].astype(F32))
    g1 = _sigmoid(gl_ref[:, D_MODEL:].astype(F32))
    merged = g0 * _dot(ya_ref[...], wa_ref[...]) + g1 * _dot(ys_ref[...], ws_ref[...])
    h = x_ref[...] + _dot(merged.astype(BF16), wo_ref[...])
    h_ref[...] = h
    ms = jnp.mean(h * h, axis=-1, keepdims=True)
    u = (h * lax.rsqrt(ms + EPS) * g_ref[...]).astype(BF16)
    u_ref[...] = u
    logits = _dot(u, wr_ref[...]) + br_ref[...]
    lane = lax.broadcasted_iota(jnp.int32, logits.shape, 1).astype(F32)
    idx_out = jnp.zeros(logits.shape, F32)
    val_out = jnp.zeros(logits.shape, F32)
    chosen = jnp.zeros(logits.shape, F32)
    top = None
    den = None
    for k in range(TOP_K):
        m = jnp.max(logits, axis=-1, keepdims=True)
        sel = jnp.min(jnp.where(logits == m, lane, float(LANES)), axis=-1, keepdims=True)
        if k == 0:
            top = m
        e = jnp.exp(m - top)
        den = e if k == 0 else den + e
        idx_out = jnp.where(lane == k, sel, idx_out)
        val_out = jnp.where(lane == k, e, val_out)
        chosen = jnp.where(lane == sel, 1.0, chosen)
        logits = jnp.where(lane == sel, -jnp.inf, logits)
    idx_ref[...] = idx_out.astype(jnp.int32)
    tw_ref[...] = val_out / den

    @pl.when(pl.program_id(0) == 0)
    def _():
        cnt_ref[...] = jnp.zeros_like(cnt_ref)

    cnt_ref[...] += jnp.sum(chosen, axis=0, keepdims=True)


def _merge(ya, ys, gl, x2d, wa, ws, wo, g, wr, br, tm):
    rows = x2d.shape[0]
    row = lambda i: (i, 0)
    return pl.pallas_call(
        _merge_kernel,
        grid=(rows // tm,),
        in_specs=[pl.BlockSpec((tm, Q_DIM), row), pl.BlockSpec((tm, D_INNER), row),
                  pl.BlockSpec((tm, 2 * D_MODEL), row), pl.BlockSpec((tm, D_MODEL), row),
                  _const_spec((Q_DIM, D_MODEL)), _const_spec((D_INNER, D_MODEL)),
                  _const_spec((D_MODEL, D_MODEL)), _const_spec((1, D_MODEL)),
                  _const_spec((D_MODEL, LANES)), _const_spec((1, LANES))],
        out_specs=[pl.BlockSpec((tm, D_MODEL), row), pl.BlockSpec((tm, D_MODEL), row),
                   pl.BlockSpec((tm, LANES), row), pl.BlockSpec((tm, LANES), row),
                   pl.BlockSpec((8, LANES), lambda i: (0, 0))],
        out_shape=[jax.ShapeDtypeStruct((rows, D_MODEL), F32),
                   jax.ShapeDtypeStruct((rows, D_MODEL), BF16),
                   jax.ShapeDtypeStruct((rows, LANES), jnp.int32),
                   jax.ShapeDtypeStruct((rows, LANES), F32),
                   jax.ShapeDtypeStruct((8, LANES), F32)],
        compiler_params=pltpu.CompilerParams(dimension_semantics=("arbitrary",),
                                             vmem_limit_bytes=VMEM_LIMIT),
        name="merge_router",
    )(ya, ys, gl, x2d, wa, ws, wo, g, wr, br)


def _route_kernel(idx_ref, cnt_ref, dest_ref, info_ref, base_ref, *, tm_e):
    i = pl.program_id(0)
    tr = idx_ref.shape[0]
    lane = lax.broadcasted_iota(jnp.int32, (tr, LANES), 1)
    idx = idx_ref[...]
    onehot = [jnp.where(lane == idx[:, k:k + 1], 1.0, 0.0) for k in range(TOP_K)]
    oh_all = onehot[0] + onehot[1] + onehot[2] + onehot[3]

    @pl.when(i == 0)
    def _():
        counts = cnt_ref[...]
        padded = jnp.floor((counts + (tm_e - 1)) * (1.0 / tm_e)) * tm_e
        lane8 = lax.broadcasted_iota(jnp.int32, (8, LANES), 1)
        incl = padded
        shift = 1
        while shift < LANES:
            incl = incl + jnp.where(lane8 >= shift, pltpu.roll(incl, shift, 1), 0.0)
            shift *= 2
        pstart = incl - padded
        base_ref[...] = pstart[0:1]
        row8 = lax.broadcasted_iota(jnp.int32, (8, LANES), 0)
        info = jnp.where(row8 == 0, counts, jnp.where(row8 == 1, pstart, incl))
        info_ref[...] = info.astype(jnp.int32)

    r = lax.broadcasted_iota(jnp.int32, (tr, tr), 0)
    c = lax.broadcasted_iota(jnp.int32, (tr, tr), 1)
    before = jnp.where(r > c, 1.0, 0.0).astype(BF16)
    earlier = _dot(before, oh_all.astype(BF16))
    rank = base_ref[...] + earlier
    dest = jnp.zeros((tr, LANES), F32)
    for k in range(TOP_K):
        d_k = jnp.sum(onehot[k] * rank, axis=-1, keepdims=True)
        dest = jnp.where(lane == k, d_k, dest)
    dest_ref[...] = dest.T[0:8].astype(jnp.int32)
    base_ref[...] += jnp.sum(oh_all, axis=0, keepdims=True)


def _route(top_idx, counts, tm_e, tr):
    rows = top_idx.shape[0]
    return pl.pallas_call(
        functools.partial(_route_kernel, tm_e=tm_e),
        grid=(rows // tr,),
        in_specs=[pl.BlockSpec((tr, LANES), lambda i: (i, 0)), _const_spec((8, LANES))],
        out_specs=[pl.BlockSpec((8, tr), lambda i: (0, i)),
                   pl.BlockSpec((8, LANES), lambda i: (0, 0))],
        out_shape=[jax.ShapeDtypeStruct((8, rows), jnp.int32),
                   jax.ShapeDtypeStruct((8, LANES), jnp.int32)],
        scratch_shapes=[pltpu.VMEM((1, LANES), F32)],
        compiler_params=pltpu.CompilerParams(dimension_semantics=("arbitrary",),
                                             vmem_limit_bytes=VMEM_LIMIT),
        name="route",
    )(top_idx, counts)


FF_CHUNK = 2 * LANES


def _moe_kernel(te_ref, nv_ref, x_ref, w1_ref, b1_ref, w2_ref, b2_ref, y_ref, w1s_ref, w2s_ref):
    s = pl.program_id(0)
    n = pl.num_programs(0) - 1
    tile = s - 1
    valid = jnp.logical_and(s >= 1, tile < nv_ref[0])
    prev = te_ref[jnp.maximum(tile - 1, 0)]
    cur = te_ref[jnp.maximum(tile, 0)]
    nxt = te_ref[jnp.minimum(s, n - 1)]

    @pl.when(jnp.logical_and(valid, jnp.logical_or(tile == 0, cur != prev)))
    def _():
        w2s_ref[...] = w2_ref[0].astype(BF16)

    @pl.when(valid)
    def _():
        x = x_ref[...]
        acts = []
        for c0 in range(0, 2 * D_FF, FF_CHUNK):
            cs = slice(c0, c0 + FF_CHUNK)
            hh = _dot(x, w1s_ref[:, cs]) + b1_ref[0, :, cs]
            glu = jnp.minimum(hh[:, :LANES], SWIGLU_LIMIT)
            lin = jnp.clip(hh[:, LANES:], -SWIGLU_LIMIT, SWIGLU_LIMIT)
            acts.append((glu * _sigmoid(SWIGLU_ALPHA * glu) * (lin + 1.0)).astype(BF16))
        act = jnp.concatenate(acts, axis=1)
        y_ref[...] = (_dot(act, w2s_ref[...]) + b2_ref[0]).astype(BF16)

    @pl.when(jnp.logical_and(s >= 1, jnp.logical_not(valid)))
    def _():
        y_ref[...] = jnp.zeros_like(y_ref)

    next_valid = jnp.logical_and(s < n, s < nv_ref[0])
    @pl.when(jnp.logical_and(next_valid, jnp.logical_or(s == 0, nxt != cur)))
    def _():
        r = lax.broadcasted_iota(jnp.int32, (FF_CHUNK, FF_CHUNK), 0)
        c = lax.broadcasted_iota(jnp.int32, (FF_CHUNK, FF_CHUNK), 1)
        src = jnp.where(c < LANES, 2 * c, 2 * (c - LANES) + 1)
        perm = jnp.where(r == src, 1.0, 0.0).astype(BF16)
        for c0 in range(0, 2 * D_FF, FF_CHUNK):
            cs = slice(c0, c0 + FF_CHUNK)
            w1s_ref[:, cs] = _dot(w1_ref[0, :, cs].astype(BF16), perm).astype(BF16)


def _moe_kernel_into(te_ref, nv_ref, x_ref, w1_ref, b1_ref, w2_ref, b2_ref, yprev_ref, y_ref,
                     w1s_ref, w2s_ref):
    del yprev_ref
    _moe_kernel(te_ref, nv_ref, x_ref, w1_ref, b1_ref, w2_ref, b2_ref, y_ref, w1s_ref, w2s_ref)


def _moe(tile_e, n_valid, xrows, w1, b1, w2, b2, tm, tile0, n_tiles_total, y_prev=None):
    n_tiles = xrows.shape[0] // tm
    tile = lambda s: jnp.maximum(s - 1, 0)
    wsel = lambda s, te, nv: (te[tile(s)], 0, 0)
    wsel_ahead = lambda s, te, nv: (te[jnp.minimum(s, n_tiles - 1)], 0, 0)
    in_specs = [pl.BlockSpec((tm, D_MODEL), lambda s, te, nv: (tile(s), 0)),
                pl.BlockSpec((1, D_MODEL, 2 * D_FF), wsel_ahead),
                pl.BlockSpec((1, 1, 2 * D_FF), wsel),
                pl.BlockSpec((1, D_FF, D_MODEL), wsel), pl.BlockSpec((1, 1, D_MODEL), wsel)]
    args = [tile_e, n_valid, xrows, w1, b1, w2, b2]
    if y_prev is not None:
        in_specs.append(pl.BlockSpec(memory_space=pl.ANY))
        args.append(y_prev)
    grid_spec = pltpu.PrefetchScalarGridSpec(
        num_scalar_prefetch=2,
        grid=(n_tiles + 1,),
        in_specs=in_specs,
        out_specs=pl.BlockSpec((tm, D_MODEL), lambda s, te, nv: (tile(s) + tile0, 0)),
        scratch_shapes=[pltpu.VMEM((D_MODEL, 2 * D_FF), BF16), pltpu.VMEM((D_FF, D_MODEL), BF16)],
    )
    return pl.pallas_call(
        _moe_kernel if y_prev is None else _moe_kernel_into,
        grid_spec=grid_spec,
        out_shape=jax.ShapeDtypeStruct((n_tiles_total * tm, D_MODEL), BF16),
        input_output_aliases={} if y_prev is None else {len(args) - 1: 0},
        compiler_params=pltpu.CompilerParams(dimension_semantics=("arbitrary",),
                                             vmem_limit_bytes=VMEM_LIMIT),
        name="moe_experts",
    )(*args)


def _final_kernel(h_ref, y0_ref, y1_ref, y2_ref, y3_ref, tw_ref, g_ref, o_ref):
    h = h_ref[...]
    tw = tw_ref[...]
    for k, y_ref in enumerate((y0_ref, y1_ref, y2_ref, y3_ref)):
        h = h + tw[:, k:k + 1] * y_ref[...].astype(F32)
    ms = jnp.mean(h * h, axis=-1, keepdims=True)
    o_ref[...] = h * lax.rsqrt(ms + EPS) * g_ref[...]


def _final(h, yg, tw, g, tm):
    rows = h.shape[0]
    row = lambda i: (i, 0)
    return pl.pallas_call(
        _final_kernel,
        grid=(rows // tm,),
        in_specs=[pl.BlockSpec((tm, D_MODEL), row)] * (1 + TOP_K)
        + [pl.BlockSpec((tm, LANES), row), _const_spec((1, D_MODEL))],
        out_specs=pl.BlockSpec((tm, D_MODEL), row),
        out_shape=jax.ShapeDtypeStruct((rows, D_MODEL), F32),
        compiler_params=pltpu.CompilerParams(dimension_semantics=("parallel",),
                                             vmem_limit_bytes=VMEM_LIMIT),
        name="combine_final_norm",
    )(h, *yg, tw, g)


def _t5_bucket(dist):
    n = jnp.maximum(dist, 0)
    max_exact = N_BUCKETS // 2
    nf = jnp.maximum(n, 1).astype(F32)
    large = max_exact + (jnp.log(nf / max_exact) / math.log(MAX_DISTANCE / max_exact)
                         * (N_BUCKETS - max_exact)).astype(jnp.int32)
    large = jnp.minimum(large, N_BUCKETS - 1)
    return jnp.where(n < max_exact, n, large)


def _bias_tables(rel_bias, nb):
    table = rel_bias.astype(F32) * LOG2E
    j = jnp.arange(BLK, dtype=jnp.int32)[:, None]
    s = jnp.arange(BLK, dtype=jnp.int32)[None, :]
    from_prev = s > j
    dist = jnp.where(from_prev, j + BLK - s, j - s)

    def lookup(bucket):
        onehot = (bucket[..., None] == jnp.arange(N_BUCKETS, dtype=jnp.int32)).astype(F32)
        return jnp.einsum('...b,bh->...h', onehot, table, precision=lax.Precision.HIGHEST)

    win = lookup(_t5_bucket(dist))
    win = jnp.stack([jnp.where(from_prev[:, :, None], NEG, win), win])
    n_idx = jnp.arange(nb, dtype=jnp.int32)[:, None, None]
    m = jnp.arange(N_META, dtype=jnp.int32)[None, None, :]
    dist_m = N_META + n_idx * BLK + j[None] - m
    meta = lookup(_t5_bucket(dist_m))
    meta = jnp.pad(meta, ((0, 0), (0, 0), (0, LANES - N_META), (0, 0)), constant_values=NEG)
    return win.transpose(0, 3, 1, 2), meta.transpose(0, 3, 1, 2)


def _dup_heads(w):
    w = w.reshape(w.shape[0], N_KV_HEADS, 1, HEAD_DIM)
    return jnp.broadcast_to(w, (w.shape[0], N_KV_HEADS, 2, HEAD_DIM)).reshape(w.shape[0], KV_DUP)


def _pad_lanes(v, value=0.0):
    v = v.reshape(1, -1).astype(F32)
    return jnp.pad(v, ((0, 0), (0, LANES - v.shape[1])), constant_values=value)


def _head_rows(v):
    return jnp.broadcast_to(v.astype(F32)[:, None], (N_SSM_HEADS, BLK))


def _row_tile(rows, target):
    tm = min(rows, target)
    assert rows % tm == 0
    return tm


def kernel(x, meta_tokens, rel_bias, norm_mix_g, w_in, attn_sinks, conv_w, conv_b, dt_bias, a_log,
           d_skip, ssm_norm_g, w_attn_br, w_ssm_br, w_out, norm_ffn_g, w_router, b_router,
           w_exp_in, b_exp_in, w_exp_out, b_exp_out, norm_final_g):
    bsz, seq, _ = x.shape
    assert seq % BLK == 0 and w_in.shape[0] == 1
    nb = seq // BLK
    rows = bsz * seq
    x2d = x.reshape(rows, D_MODEL)

    wi = w_in[0]
    o_k, o_v, o_z = Q_DIM, Q_DIM + KV_DIM, Q_DIM + 2 * KV_DIM
    o_x, o_dt = o_z + D_INNER, o_z + D_INNER + CONV_DIM
    o_g = o_dt + N_SSM_HEADS
    w_proj = jnp.concatenate([
        wi[:, :o_k] * (HEAD_DIM ** -0.5 * LOG2E), _dup_heads(wi[:, o_k:o_v]),
        _dup_heads(wi[:, o_v:o_z]),
        wi[:, o_z:o_x], wi[:, o_x:o_dt], wi[:, o_g:]], axis=1).astype(BF16)
    w_dt = jnp.pad(wi[:, o_dt:o_g], ((0, 0), (0, LANES - N_SSM_HEADS))).astype(BF16)
    g_mix = norm_mix_g[0].reshape(1, D_MODEL)

    tm = _row_tile(rows, ROW_TILE)
    q, k, v, z, xbc, gate, dtraw = _inproj(x2d, g_mix, w_proj, w_dt, tm)
    _, k_m, v_m, z_m, xbc_m, _, dtraw_m = _inproj(meta_tokens.astype(F32), g_mix, w_proj, w_dt,
                                                  N_META)

    pad_meta = ((BLK - N_META, 0), (0, 0))
    pad_meta_tail = ((0, BLK - N_META), (0, 0))
    bias_win, bias_meta = _bias_tables(rel_bias, nb)
    y_attn = _attention(q, k, v, jnp.pad(k_m, pad_meta_tail), jnp.pad(v_m, pad_meta_tail),
                        bias_win, bias_meta, attn_sinks[0].astype(F32) * LOG2E, bsz, nb)

    ssd_params = (conv_w[0].astype(F32), conv_b[0].reshape(1, CONV_DIM).astype(F32),
                  _head_rows(dt_bias[0]), _head_rows(a_log[0]),
                  jnp.repeat(d_skip[0].astype(F32), SSM_HEAD_DIM).reshape(1, D_INNER),
                  ssm_norm_g[0].reshape(1, D_INNER).astype(F32))
    xbc_mp = jnp.pad(xbc_m, pad_meta)
    zero_state = jnp.zeros((N_SSM_PAIRS, D_STATE, LANES), F32)
    _, h_meta = _ssd(xbc_mp, jnp.pad(z_m, pad_meta), jnp.pad(dtraw_m, pad_meta),
                     jnp.zeros((TAIL, CONV_DIM), BF16), zero_state, *ssd_params, 1, 1,
                     BLK - N_META)
    y_ssm, _ = _ssd(xbc, z, dtraw, xbc_mp[BLK - TAIL:], h_meta[0], *ssd_params, bsz, nb, 0)

    w_r = jnp.pad(w_router[0], ((0, 0), (0, LANES - N_EXPERTS))).astype(BF16)
    b_r = _pad_lanes(b_router[0], NEG)
    h1, u2, top_idx, top_w, counts8 = _merge(
        y_attn, y_ssm, gate, x2d, w_attn_br[0].astype(BF16), w_ssm_br[0].astype(BF16),
        w_out[0].astype(BF16), norm_ffn_g[0].reshape(1, D_MODEL), w_r, b_r, tm)

    tm_e = EXPERT_TILE
    n_tiles = -(-(rows * TOP_K + N_EXPERTS * (tm_e - 1)) // tm_e)
    dest_t, info = _route(top_idx, counts8, tm_e, _row_tile(rows, ROUTE_TILE))
    counts, pstart, pend = info[0, :N_EXPERTS], info[1, :N_EXPERTS], info[2, :N_EXPERTS]
    tile_start = jnp.arange(n_tiles, dtype=jnp.int32) * tm_e
    tile_e = jnp.minimum(jnp.sum(tile_start[:, None] >= pend[None, :], axis=1),
                         N_EXPERTS - 1).astype(jnp.int32)
    n_valid = pend[-1:] // tm_e
    n_rows = n_tiles * tm_e
    n_assign = rows * TOP_K
    slot = jnp.arange(tm_e, dtype=jnp.int32)[None, :]
    pad_key = (pstart + counts)[:, None] + slot
    pad_key = jnp.where(pad_key < pend[:, None], pad_key, n_rows).reshape(-1)
    n_fill = max(pad_key.shape[0], n_rows - n_assign)
    pad_key = jnp.pad(pad_key, (0, n_fill - pad_key.shape[0]), constant_values=n_rows)
    keys = jnp.concatenate([dest_t[:TOP_K].reshape(-1), pad_key])
    toks = jnp.concatenate([jnp.tile(jnp.arange(rows, dtype=jnp.int32), TOP_K),
                            jnp.arange(n_fill, dtype=jnp.int32) % rows])
    row_tok = lax.sort((keys, toks), num_keys=1)[1][:n_rows]

    b1 = b_exp_in[0].astype(F32).reshape(N_EXPERTS, 2 * D_FF // FF_CHUNK, LANES, 2)
    b1 = b1.transpose(0, 1, 3, 2).reshape(N_EXPERTS, 1, 2 * D_FF)
    b2 = b_exp_out[0].reshape(N_EXPERTS, 1, D_MODEL).astype(F32)
    y_rows = None
    bounds = [n_tiles * p // MOE_CALLS for p in range(MOE_CALLS + 1)]
    for t0, t1 in zip(bounds[:-1], bounds[1:]):
        y_rows = _moe(tile_e[t0:t1], jnp.clip(n_valid - t0, 0, t1 - t0),
                      u2[row_tok[t0 * tm_e:t1 * tm_e]], w_exp_in[0], b1, w_exp_out[0], b2, tm_e,
                      t0, n_tiles, y_rows)

    yg = [y_rows[dest_t[k]] for k in range(TOP_K)]
    out = _final(h1, yg, top_w, norm_final_g.reshape(1, D_MODEL).astype(F32), tm)
    return out.reshape(bsz, seq, D_MODEL)
```

```python
import functools
import math

import jax
import jax.numpy as jnp
from jax import lax
from jax.experimental import pallas as pl
from jax.experimental.pallas import tpu as pltpu

F32 = jnp.float32
BF16 = jnp.bfloat16

D_MODEL = 1024
N_META = 16
N_HEADS = 16
N_KV_HEADS = 4
HEAD_DIM = 64
Q_PER_KV = N_HEADS // N_KV_HEADS
Q_DIM = N_HEADS * HEAD_DIM
KV_DIM = N_KV_HEADS * HEAD_DIM
WINDOW = 128
BLK = 128
N_BUCKETS = 32
MAX_DISTANCE = 128
D_INNER = 2 * D_MODEL
SSM_HEAD_DIM = 64
N_SSM_HEADS = D_INNER // SSM_HEAD_DIM
N_GROUPS = 4
D_STATE = 128
CONV_WIDTH = 4
BC_DIM = N_GROUPS * D_STATE
CONV_DIM = D_INNER + 2 * BC_DIM
N_EXPERTS = 32
TOP_K = 4
D_FF = D_MODEL
SWIGLU_ALPHA = 1.702
SWIGLU_LIMIT = 7.0
EPS = 1e-5
NEG = -1e30
LOG2E = math.log2(math.e)

LANES = 128
KV_DUP = 2 * KV_DIM
N_PAIRS = N_HEADS // 2
N_SSM_PAIRS = N_SSM_HEADS // 2
CTX = 3 * BLK
VMEM_LIMIT = 56 * 1024 * 1024
ROW_TILE = 512
ROUTE_TILE = 1024
EXPERT_TILE = 512
MOE_SPLITS = (0, 2, 6, 11, 16)

PROJ_SEGS = (("q", Q_DIM), ("k", KV_DUP), ("v", KV_DUP), ("z", D_INNER), ("xbc", CONV_DIM),
             ("gate", 2 * D_MODEL))
PROJ_N = sum(w for _, w in PROJ_SEGS)
PROJ_CHUNK = 512


def _dot(a, b):
    return jnp.dot(a, b, preferred_element_type=F32)


def _dot_nt(a, b):
    return lax.dot_general(a, b, (((1,), (1,)), ((), ())), preferred_element_type=F32)


def _sigmoid(x):
    return 0.5 * jnp.tanh(0.5 * x) + 0.5


def _silu(x):
    h = 0.5 * x
    return h * jnp.tanh(h) + h


def _const_spec(shape):
    nd = len(shape)
    return pl.BlockSpec(shape, lambda *_: (0,) * nd, pipeline_mode=pl.Buffered(1))


def _inproj_kernel(x_ref, g_ref, w_ref, wdt_ref, q_ref, k_ref, v_ref, z_ref, xbc_ref, gate_ref,
                   dt_ref):
    x = x_ref[...]
    ms = jnp.mean(x * x, axis=-1, keepdims=True)
    u = (x * lax.rsqrt(ms + EPS) * g_ref[...]).astype(BF16)
    outs = (q_ref, k_ref, v_ref, z_ref, xbc_ref, gate_ref)
    off = 0
    for ref, (_, width) in zip(outs, PROJ_SEGS):
        for c0 in range(0, width, PROJ_CHUNK):
            ref[:, c0:c0 + PROJ_CHUNK] = _dot(
                u, w_ref[:, off + c0:off + c0 + PROJ_CHUNK]).astype(BF16)
        off += width
    dt_ref[...] = _dot(u, wdt_ref[...])


def _inproj(x2d, g, w, wdt, tm):
    rows = x2d.shape[0]
    row = lambda i: (i, 0)
    out_shape = [jax.ShapeDtypeStruct((rows, wd), BF16) for _, wd in PROJ_SEGS]
    out_shape.append(jax.ShapeDtypeStruct((rows, LANES), F32))
    out_specs = [pl.BlockSpec((tm, wd), row) for _, wd in PROJ_SEGS]
    out_specs.append(pl.BlockSpec((tm, LANES), row))
    return pl.pallas_call(
        _inproj_kernel,
        grid=(rows // tm,),
        in_specs=[pl.BlockSpec((tm, D_MODEL), row), _const_spec((1, D_MODEL)),
                  _const_spec((D_MODEL, PROJ_N)), _const_spec((D_MODEL, LANES))],
        out_specs=out_specs,
        out_shape=out_shape,
        compiler_params=pltpu.CompilerParams(dimension_semantics=("parallel",),
                                             vmem_limit_bytes=VMEM_LIMIT),
        name="inproj",
    )(x2d, g, w, wdt)


def _attn_kernel(sink_ref, q_ref, kp_ref, kc_ref, km_ref, vp_ref, vc_ref, vm_ref, bwin_ref,
                 bmeta_ref, o_ref, s_ref, p_ref, inv_ref):
    lo = lax.broadcasted_iota(jnp.int32, (1, LANES), 1) < HEAD_DIM
    from_prev = (lax.broadcasted_iota(jnp.int32, (BLK, BLK), 1)
                 > lax.broadcasted_iota(jnp.int32, (BLK, BLK), 0))
    zero = jnp.zeros((), BF16)
    pairs_per_kv = Q_PER_KV // 2

    for g in range(N_KV_HEADS):
        cs = slice(g * LANES, (g + 1) * LANES)
        kk = jnp.concatenate([kp_ref[:, cs], kc_ref[:, cs], km_ref[:, cs]], axis=0)
        k_lo = jnp.where(lo, kk, zero)
        k_hi = jnp.where(lo, zero, kk)
        for j in range(pairs_per_kv):
            pi = g * pairs_per_kv + j
            qp = q_ref[:, pi * LANES:(pi + 1) * LANES]
            s_ref[2 * pi] = _dot_nt(qp, k_lo)
            s_ref[2 * pi + 1] = _dot_nt(qp, k_hi)

    for pi in range(N_PAIRS):
        invs = []
        for t in range(2):
            h = 2 * pi + t
            s_win = (jnp.where(from_prev, s_ref[h, :, :BLK], s_ref[h, :, BLK:2 * BLK])
                     + bwin_ref[0, h])
            s_meta = s_ref[h, :, 2 * BLK:] + bmeta_ref[0, h]
            sink = sink_ref[h]
            m = jnp.maximum(jnp.max(jnp.maximum(s_win, s_meta), axis=-1, keepdims=True), sink)
            p_win = jnp.exp2(s_win - m)
            p_meta = jnp.exp2(s_meta - m)
            den = jnp.sum(p_win + p_meta, axis=-1, keepdims=True) + jnp.exp2(sink - m)
            c0 = t * CTX
            p_ref[pi, :, c0:c0 + BLK] = jnp.where(from_prev, p_win, 0.0).astype(BF16)
            p_ref[pi, :, c0 + BLK:c0 + 2 * BLK] = jnp.where(from_prev, 0.0, p_win).astype(BF16)
            p_ref[pi, :, c0 + 2 * BLK:c0 + CTX] = p_meta.astype(BF16)
            invs.append(1.0 / den)
        inv_ref[pi] = jnp.where(lo, invs[0], invs[1])

    for g in range(N_KV_HEADS):
        cs = slice(g * LANES, (g + 1) * LANES)
        vv = jnp.concatenate([vp_ref[:, cs], vc_ref[:, cs], vm_ref[:, cs]], axis=0)
        v_bd = jnp.concatenate([jnp.where(lo, vv, zero), jnp.where(lo, zero, vv)], axis=0)
        for j in range(pairs_per_kv):
            pi = g * pairs_per_kv + j
            o = _dot(p_ref[pi], v_bd) * inv_ref[pi]
            o_ref[:, pi * LANES:(pi + 1) * LANES] = o.astype(BF16)


def _attention(q, k, v, k_meta, v_meta, bias_win, bias_meta, sinks, bsz, nb):
    cur = lambda b, n, *_: (b * nb + n, 0)
    prev = lambda b, n, *_: (jnp.maximum(b * nb + n - 1, 0), 0)
    const2 = lambda b, n, *_: (0, 0)
    grid_spec = pltpu.PrefetchScalarGridSpec(
        num_scalar_prefetch=1,
        grid=(bsz, nb),
        in_specs=[
            pl.BlockSpec((BLK, Q_DIM), cur),
            pl.BlockSpec((BLK, KV_DUP), prev), pl.BlockSpec((BLK, KV_DUP), cur),
            pl.BlockSpec((BLK, KV_DUP), const2),
            pl.BlockSpec((BLK, KV_DUP), prev), pl.BlockSpec((BLK, KV_DUP), cur),
            pl.BlockSpec((BLK, KV_DUP), const2),
            pl.BlockSpec((1, N_HEADS, BLK, BLK), lambda b, n, *_: (jnp.minimum(n, 1), 0, 0, 0)),
            pl.BlockSpec((1, N_HEADS, BLK, LANES), lambda b, n, *_: (n, 0, 0, 0)),
        ],
        out_specs=pl.BlockSpec((BLK, Q_DIM), cur),
        scratch_shapes=[pltpu.VMEM((N_HEADS, BLK, CTX), F32),
                        pltpu.VMEM((N_PAIRS, BLK, 2 * CTX), BF16),
                        pltpu.VMEM((N_PAIRS, BLK, LANES), F32)],
    )
    return pl.pallas_call(
        _attn_kernel,
        grid_spec=grid_spec,
        out_shape=jax.ShapeDtypeStruct((bsz * nb * BLK, Q_DIM), BF16),
        compiler_params=pltpu.CompilerParams(dimension_semantics=("parallel", "arbitrary"),
                                             vmem_limit_bytes=VMEM_LIMIT),
        name="swa_attention",
    )(sinks, q, k, k, k_meta, v, v, v_meta, bias_win, bias_meta)


CONV_COLS = 256
TAIL = 16


def _ssd_kernel(xbc_ref, z_ref, dtraw_ref, tail0_ref, h0_ref, convw_ref, convb_ref, dtb_ref,
                alog_ref, dskip_ref, ng_ref, y_ref, hfin_ref, win_ref, xc_ref, st_ref, *, n_pad):
    for s in range(xbc_ref.shape[0]):
        _ssd_chunk(xbc_ref.at[s], z_ref.at[s], dtraw_ref.at[s], tail0_ref, h0_ref, convw_ref,
                   convb_ref, dtb_ref, alog_ref, dskip_ref, ng_ref, y_ref.at[s], hfin_ref.at[s],
                   win_ref.at[s], xc_ref.at[s], st_ref.at[s], n_pad)


def _ssd_chunk(xbc_ref, z_ref, dtraw_ref, tail0_ref, h0_ref, convw_ref, convb_ref, dtb_ref,
               alog_ref, dskip_ref, ng_ref, y_ref, hfin_ref, win_ref, xc_ref, st_ref, n_pad):
    c = pl.program_id(1)

    @pl.when(c == 0)
    def _():
        win_ref[0:TAIL, :] = tail0_ref[...]
        st_ref[...] = h0_ref[...]

    win_ref[TAIL:TAIL + BLK, :] = xbc_ref[...]
    sr = lax.broadcasted_iota(jnp.int32, (BLK, TAIL + BLK), 0)
    sc = lax.broadcasted_iota(jnp.int32, (BLK, TAIL + BLK), 1)
    shifts = [jnp.where(sc == sr + (TAIL - (CONV_WIDTH - 1) + w), 1.0, 0.0).astype(BF16)
              for w in range(CONV_WIDTH)]
    live = lax.broadcasted_iota(jnp.int32, (BLK, 1), 0) >= n_pad
    for c0 in range(0, CONV_DIM, CONV_COLS):
        cs = slice(c0, c0 + CONV_COLS)
        wv = win_ref[:, cs]
        acc = convb_ref[:, cs]
        for w in range(CONV_WIDTH):
            acc = acc + convw_ref[w:w + 1, cs] * _dot(shifts[w], wv)
        act = _silu(acc)
        if n_pad:
            act = jnp.where(live, act, 0.0)
        xc_ref[:, cs] = act
    win_ref[0:TAIL, :] = win_ref[BLK:BLK + TAIL, :]

    lane_t = lax.broadcasted_iota(jnp.int32, (N_SSM_HEADS, BLK), 1)
    x_t = dtraw_ref[...].T[:N_SSM_HEADS] + dtb_ref[...]
    dt_t = jnp.maximum(x_t, 0.0) + jnp.log1p(jnp.exp(-jnp.abs(x_t)))
    if n_pad:
        dt_t = jnp.where(lane_t >= n_pad, dt_t, 0.0)
    a_cs_t = dt_t * (-jnp.exp(alog_ref[...]))
    shift = 1
    while shift < BLK:
        a_cs_t = a_cs_t + jnp.where(lane_t >= shift, pltpu.roll(a_cs_t, shift, 1), 0.0)
        shift *= 2
    w_t = dt_t * jnp.exp(a_cs_t[:, BLK - 1:BLK] - a_cs_t)
    src_t = a_cs_t - jnp.log(dt_t)
    fill = jnp.zeros((BLK - N_SSM_HEADS, BLK), F32)
    a_cs = jnp.concatenate([a_cs_t, fill], axis=0).T
    e_cs = jnp.exp(a_cs)
    ri = lax.broadcasted_iota(jnp.int32, (BLK, BLK), 0)
    ci = lax.broadcasted_iota(jnp.int32, (BLK, BLK), 1)
    causal = ri >= ci
    lo = lax.broadcasted_iota(jnp.int32, (1, LANES), 1) < SSM_HEAD_DIM

    pairs_per_group = N_SSM_PAIRS // N_GROUPS
    for g in range(N_GROUPS):
        b_g = xc_ref[:, D_INNER + g * D_STATE:D_INNER + (g + 1) * D_STATE]
        c_g = xc_ref[:, D_INNER + BC_DIM + g * D_STATE:D_INNER + BC_DIM + (g + 1) * D_STATE]
        cb = _dot_nt(c_g.astype(BF16), b_g.astype(BF16))
        b_gt = b_g.T
        ys = []
        for j in range(pairs_per_group):
            pi = g * pairs_per_group + j
            ps = slice(pi * LANES, (pi + 1) * LANES)
            xs = xc_ref[:, ps]
            xs_b = xs.astype(BF16)
            st = st_ref[pi]
            rhs = jnp.concatenate([xs_b, st.astype(BF16)], axis=0)
            y_h, st_h = [], []
            for t in range(2):
                h = 2 * pi + t
                diff = a_cs[:, h:h + 1] - src_t[h:h + 1, :]
                m_h = cb * jnp.exp(jnp.where(causal, diff, -jnp.inf))
                ce = c_g * e_cs[:, h:h + 1]
                lhs = jnp.concatenate([m_h, ce], axis=1).astype(BF16)
                y_h.append(_dot(lhs, rhs))
                s_new = _dot((b_gt * w_t[h:h + 1, :]).astype(BF16), xs_b)
                st_h.append(st * e_cs[BLK - 1:BLK, h:h + 1] + s_new)
            st_ref[pi] = jnp.where(lo, st_h[0], st_h[1])
            ys.append(jnp.where(lo, y_h[0], y_h[1]) + dskip_ref[:, ps] * xs)
        gs = slice(g * (D_INNER // N_GROUPS), (g + 1) * (D_INNER // N_GROUPS))
        zg = z_ref[:, gs].astype(F32)
        yg = jnp.concatenate(ys, axis=1) * _silu(zg)
        ms = jnp.mean(yg * yg, axis=-1, keepdims=True)
        y_ref[:, gs] = (yg * lax.rsqrt(ms + EPS) * ng_ref[:, gs]).astype(BF16)

    @pl.when(c == pl.num_programs(1) - 1)
    def _():
        hfin_ref[0] = st_ref[...]


def _ssd(xbc, z, dtraw, tail0, h0, convw, convb, dtb, alog, dskip, ng, bsz, nc, n_pad):
    n_seq = 2 if bsz % 2 == 0 else 1
    nbg = bsz // n_seq
    rows_g = nbg * nc * BLK
    cur = lambda b, c: (0, b * nc + c, 0)
    split = lambda a: a.reshape(n_seq, rows_g, a.shape[-1])
    y, h_fin = pl.pallas_call(
        functools.partial(_ssd_kernel, n_pad=n_pad),
        grid=(nbg, nc),
        in_specs=[
            pl.BlockSpec((n_seq, BLK, CONV_DIM), cur), pl.BlockSpec((n_seq, BLK, D_INNER), cur),
            pl.BlockSpec((n_seq, BLK, LANES), cur),
            _const_spec((TAIL, CONV_DIM)), _const_spec((N_SSM_PAIRS, D_STATE, LANES)),
            _const_spec((CONV_WIDTH, CONV_DIM)), _const_spec((1, CONV_DIM)),
            _const_spec((N_SSM_HEADS, BLK)), _const_spec((N_SSM_HEADS, BLK)),
            _const_spec((1, D_INNER)), _const_spec((1, D_INNER)),
        ],
        out_specs=[pl.BlockSpec((n_seq, BLK, D_INNER), cur),
                   pl.BlockSpec((n_seq, 1, N_SSM_PAIRS, D_STATE, LANES),
                                lambda b, c: (0, b, 0, 0, 0))],
        out_shape=[jax.ShapeDtypeStruct((n_seq, rows_g, D_INNER), BF16),
                   jax.ShapeDtypeStruct((n_seq, nbg, N_SSM_PAIRS, D_STATE, LANES), F32)],
        scratch_shapes=[pltpu.VMEM((n_seq, TAIL + BLK, CONV_DIM), BF16),
                        pltpu.VMEM((n_seq, BLK, CONV_DIM), F32),
                        pltpu.VMEM((n_seq, N_SSM_PAIRS, D_STATE, LANES), F32)],
        compiler_params=pltpu.CompilerParams(dimension_semantics=("parallel", "arbitrary"),
                                             vmem_limit_bytes=VMEM_LIMIT),
        name="ssd_mixer",
    )(split(xbc), split(z), split(dtraw), tail0, h0, convw, convb, dtb, alog, dskip, ng)
    return (y.reshape(bsz * nc * BLK, D_INNER),
            h_fin.reshape(bsz, N_SSM_PAIRS, D_STATE, LANES))


def _merge_kernel(ya_ref, ys_ref, gl_ref, x_ref, wa_ref, ws_ref, wo_ref, g_ref, wr_ref, br_ref,
                  h_ref, u_ref, idx_ref, tw_ref, cnt_ref):
    g0 = _sigmoid(gl_ref[:, :D_MODEL].astype(F32))
    g1 = _sigmoid(gl_ref[:, D_MODEL:].astype(F32))
    merged = g0 * _dot(ya_ref[...], wa_ref[...]) + g1 * _dot(ys_ref[...], ws_ref[...])
    h = x_ref[...] + _dot(merged.astype(BF16), wo_ref[...])
    h_ref[...] = h
    ms = jnp.mean(h * h, axis=-1, keepdims=True)
    u = (h * lax.rsqrt(ms + EPS) * g_ref[...]).astype(BF16)
    u_ref[...] = u
    logits = _dot(u, wr_ref[...]) + br_ref[...]
    lane = lax.broadcasted_iota(jnp.int32, logits.shape, 1).astype(F32)
    idx_out = jnp.zeros(logits.shape, F32)
    val_out = jnp.zeros(logits.shape, F32)
    chosen = jnp.zeros(logits.shape, F32)
    top = None
    den = None
    for k in range(TOP_K):
        m = jnp.max(logits, axis=-1, keepdims=True)
        sel = jnp.min(jnp.where(logits == m, lane, float(LANES)), axis=-1, keepdims=True)
        if k == 0:
            top = m
        e = jnp.exp(m - top)
        den = e if k == 0 else den + e
        idx_out = jnp.where(lane == k, sel, idx_out)
        val_out = jnp.where(lane == k, e, val_out)
        chosen = jnp.where(lane == sel, 1.0, chosen)
        logits = jnp.where(lane == sel, -jnp.inf, logits)
    idx_ref[...] = idx_out.astype(jnp.int32)
    tw_ref[...] = val_out / den

    @pl.when(pl.program_id(0) == 0)
    def _():
        cnt_ref[...] = jnp.zeros_like(cnt_ref)

    cnt_ref[...] += jnp.sum(chosen, axis=0, keepdims=True)


def _merge(ya, ys, gl, x2d, wa, ws, wo, g, wr, br, tm):
    rows = x2d.shape[0]
    row = lambda i: (i, 0)
    return pl.pallas_call(
        _merge_kernel,
        grid=(rows // tm,),
        in_specs=[pl.BlockSpec((tm, Q_DIM), row), pl.BlockSpec((tm, D_INNER), row),
                  pl.BlockSpec((tm, 2 * D_MODEL), row), pl.BlockSpec((tm, D_MODEL), row),
                  _const_spec((Q_DIM, D_MODEL)), _const_spec((D_INNER, D_MODEL)),
                  _const_spec((D_MODEL, D_MODEL)), _const_spec((1, D_MODEL)),
                  _const_spec((D_MODEL, LANES)), _const_spec((1, LANES))],
        out_specs=[pl.BlockSpec((tm, D_MODEL), row), pl.BlockSpec((tm, D_MODEL), row),
                   pl.BlockSpec((tm, LANES), row), pl.BlockSpec((tm, LANES), row),
                   pl.BlockSpec((8, LANES), lambda i: (0, 0))],
        out_shape=[jax.ShapeDtypeStruct((rows, D_MODEL), F32),
                   jax.ShapeDtypeStruct((rows, D_MODEL), BF16),
                   jax.ShapeDtypeStruct((rows, LANES), jnp.int32),
                   jax.ShapeDtypeStruct((rows, LANES), F32),
                   jax.ShapeDtypeStruct((8, LANES), F32)],
        compiler_params=pltpu.CompilerParams(dimension_semantics=("arbitrary",),
                                             vmem_limit_bytes=VMEM_LIMIT),
        name="merge_router",
    )(ya, ys, gl, x2d, wa, ws, wo, g, wr, br)


def _route_kernel(idx_ref, cnt_ref, dest_ref, info_ref, base_ref, *, tm_e):
    i = pl.program_id(0)
    tr = idx_ref.shape[0]
    lane = lax.broadcasted_iota(jnp.int32, (tr, LANES), 1)
    idx = idx_ref[...]
    onehot = [jnp.where(lane == idx[:, k:k + 1], 1.0, 0.0) for k in range(TOP_K)]
    oh_all = onehot[0] + onehot[1] + onehot[2] + onehot[3]

    @pl.when(i == 0)
    def _():
        counts = cnt_ref[...]
        padded = jnp.floor((counts + (tm_e - 1)) * (1.0 / tm_e)) * tm_e
        lane8 = lax.broadcasted_iota(jnp.int32, (8, LANES), 1)
        incl = padded
        shift = 1
        while shift < LANES:
            incl = incl + jnp.where(lane8 >= shift, pltpu.roll(incl, shift, 1), 0.0)
            shift *= 2
        pstart = incl - padded
        base_ref[...] = pstart[0:1]
        row8 = lax.broadcasted_iota(jnp.int32, (8, LANES), 0)
        info = jnp.where(row8 == 0, counts, jnp.where(row8 == 1, pstart, incl))
        info_ref[...] = info.astype(jnp.int32)

    r = lax.broadcasted_iota(jnp.int32, (tr, tr), 0)
    c = lax.broadcasted_iota(jnp.int32, (tr, tr), 1)
    before = jnp.where(r > c, 1.0, 0.0).astype(BF16)
    earlier = _dot(before, oh_all.astype(BF16))
    rank = base_ref[...] + earlier
    dest = jnp.zeros((tr, LANES), F32)
    for k in range(TOP_K):
        d_k = jnp.sum(onehot[k] * rank, axis=-1, keepdims=True)
        dest = jnp.where(lane == k, d_k, dest)
    dest_ref[...] = dest.T[0:8].astype(jnp.int32)
    base_ref[...] += jnp.sum(oh_all, axis=0, keepdims=True)


def _route(top_idx, counts, tm_e, tr):
    rows = top_idx.shape[0]
    return pl.pallas_call(
        functools.partial(_route_kernel, tm_e=tm_e),
        grid=(rows // tr,),
        in_specs=[pl.BlockSpec((tr, LANES), lambda i: (i, 0)), _const_spec((8, LANES))],
        out_specs=[pl.BlockSpec((8, tr), lambda i: (0, i)),
                   pl.BlockSpec((8, LANES), lambda i: (0, 0))],
        out_shape=[jax.ShapeDtypeStruct((8, rows), jnp.int32),
                   jax.ShapeDtypeStruct((8, LANES), jnp.int32)],
        scratch_shapes=[pltpu.VMEM((1, LANES), F32)],
        compiler_params=pltpu.CompilerParams(dimension_semantics=("arbitrary",),
                                             vmem_limit_bytes=VMEM_LIMIT),
        name="route",
    )(top_idx, counts)


FF_CHUNK = 2 * LANES


def _moe_kernel(te_ref, nv_ref, x_ref, w1_ref, b1_ref, w2_ref, b2_ref, y_ref, w1s_ref, w2s_ref):
    s = pl.program_id(0)
    n = pl.num_programs(0) - 1
    tile = s - 1
    valid = jnp.logical_and(s >= 1, tile < nv_ref[0])
    prev = te_ref[jnp.maximum(tile - 1, 0)]
    cur = te_ref[jnp.maximum(tile, 0)]
    nxt = te_ref[jnp.minimum(s, n - 1)]

    @pl.when(jnp.logical_and(valid, jnp.logical_or(tile == 0, cur != prev)))
    def _():
        w2s_ref[...] = w2_ref[0].astype(BF16)

    @pl.when(valid)
    def _():
        x = x_ref[...]
        acts = []
        for c0 in range(0, 2 * D_FF, FF_CHUNK):
            cs = slice(c0, c0 + FF_CHUNK)
            hh = _dot(x, w1s_ref[:, cs]) + b1_ref[0, :, cs]
            glu = jnp.minimum(hh[:, :LANES], SWIGLU_LIMIT)
            lin = jnp.clip(hh[:, LANES:], -SWIGLU_LIMIT, SWIGLU_LIMIT)
            acts.append((glu * _sigmoid(SWIGLU_ALPHA * glu) * (lin + 1.0)).astype(BF16))
        act = jnp.concatenate(acts, axis=1)
        y_ref[...] = (_dot(act, w2s_ref[...]) + b2_ref[0]).astype(BF16)

    @pl.when(jnp.logical_and(s >= 1, jnp.logical_not(valid)))
    def _():
        y_ref[...] = jnp.zeros_like(y_ref)

    next_valid = jnp.logical_and(s < n, s < nv_ref[0])
    @pl.when(jnp.logical_and(next_valid, jnp.logical_or(s == 0, nxt != cur)))
    def _():
        r = lax.broadcasted_iota(jnp.int32, (FF_CHUNK, FF_CHUNK), 0)
        c = lax.broadcasted_iota(jnp.int32, (FF_CHUNK, FF_CHUNK), 1)
        src = jnp.where(c < LANES, 2 * c, 2 * (c - LANES) + 1)
        perm = jnp.where(r == src, 1.0, 0.0).astype(BF16)
        for c0 in range(0, 2 * D_FF, FF_CHUNK):
            cs = slice(c0, c0 + FF_CHUNK)
            w1s_ref[:, cs] = _dot(w1_ref[0, :, cs].astype(BF16), perm).astype(BF16)


def _moe_kernel_into(te_ref, nv_ref, x_ref, w1_ref, b1_ref, w2_ref, b2_ref, yprev_ref, y_ref,
                     w1s_ref, w2s_ref):
    del yprev_ref
    _moe_kernel(te_ref, nv_ref, x_ref, w1_ref, b1_ref, w2_ref, b2_ref, y_ref, w1s_ref, w2s_ref)


def _moe(tile_e, n_valid, xrows, w1, b1, w2, b2, tm, tile0, n_tiles_total, y_prev=None):
    n_tiles = xrows.shape[0] // tm
    tile = lambda s: jnp.maximum(s - 1, 0)
    wsel = lambda s, te, nv: (te[tile(s)], 0, 0)
    wsel_ahead = lambda s, te, nv: (te[jnp.minimum(s, n_tiles - 1)], 0, 0)
    in_specs = [pl.BlockSpec((tm, D_MODEL), lambda s, te, nv: (tile(s), 0)),
                pl.BlockSpec((1, D_MODEL, 2 * D_FF), wsel_ahead),
                pl.BlockSpec((1, 1, 2 * D_FF), wsel),
                pl.BlockSpec((1, D_FF, D_MODEL), wsel), pl.BlockSpec((1, 1, D_MODEL), wsel)]
    args = [tile_e, n_valid, xrows, w1, b1, w2, b2]
    if y_prev is not None:
        in_specs.append(pl.BlockSpec(memory_space=pl.ANY))
        args.append(y_prev)
    grid_spec = pltpu.PrefetchScalarGridSpec(
        num_scalar_prefetch=2,
        grid=(n_tiles + 1,),
        in_specs=in_specs,
        out_specs=pl.BlockSpec((tm, D_MODEL), lambda s, te, nv: (tile(s) + tile0, 0)),
        scratch_shapes=[pltpu.VMEM((D_MODEL, 2 * D_FF), BF16), pltpu.VMEM((D_FF, D_MODEL), BF16)],
    )
    return pl.pallas_call(
        _moe_kernel if y_prev is None else _moe_kernel_into,
        grid_spec=grid_spec,
        out_shape=jax.ShapeDtypeStruct((n_tiles_total * tm, D_MODEL), BF16),
        input_output_aliases={} if y_prev is None else {len(args) - 1: 0},
        compiler_params=pltpu.CompilerParams(dimension_semantics=("arbitrary",),
                                             vmem_limit_bytes=VMEM_LIMIT),
        name="moe_experts",
    )(*args)


def _final_kernel(h_ref, y0_ref, y1_ref, y2_ref, y3_ref, tw_ref, g_ref, o_ref):
    h = h_ref[...]
    tw = tw_ref[...]
    for k, y_ref in enumerate((y0_ref, y1_ref, y2_ref, y3_ref)):
        h = h + tw[:, k:k + 1] * y_ref[...].astype(F32)
    ms = jnp.mean(h * h, axis=-1, keepdims=True)
    o_ref[...] = h * lax.rsqrt(ms + EPS) * g_ref[...]


def _final(h, yg, tw, g, tm):
    rows = h.shape[0]
    row = lambda i: (i, 0)
    return pl.pallas_call(
        _final_kernel,
        grid=(rows // tm,),
        in_specs=[pl.BlockSpec((tm, D_MODEL), row)] * (1 + TOP_K)
        + [pl.BlockSpec((tm, LANES), row), _const_spec((1, D_MODEL))],
        out_specs=pl.BlockSpec((tm, D_MODEL), row),
        out_shape=jax.ShapeDtypeStruct((rows, D_MODEL), F32),
        compiler_params=pltpu.CompilerParams(dimension_semantics=("parallel",),
                                             vmem_limit_bytes=VMEM_LIMIT),
        name="combine_final_norm",
    )(h, *yg, tw, g)


def _t5_bucket(dist):
    n = jnp.maximum(dist, 0)
    max_exact = N_BUCKETS // 2
    nf = jnp.maximum(n, 1).astype(F32)
    large = max_exact + (jnp.log(nf / max_exact) / math.log(MAX_DISTANCE / max_exact)
                         * (N_BUCKETS - max_exact)).astype(jnp.int32)
    large = jnp.minimum(large, N_BUCKETS - 1)
    return jnp.where(n < max_exact, n, large)


def _bias_tables(rel_bias, nb):
    table = rel_bias.astype(F32) * LOG2E
    j = jnp.arange(BLK, dtype=jnp.int32)[:, None]
    s = jnp.arange(BLK, dtype=jnp.int32)[None, :]
    from_prev = s > j
    dist = jnp.where(from_prev, j + BLK - s, j - s)

    def lookup(bucket):
        onehot = (bucket[..., None] == jnp.arange(N_BUCKETS, dtype=jnp.int32)).astype(F32)
        return jnp.einsum('...b,bh->...h', onehot, table, precision=lax.Precision.HIGHEST)

    win = lookup(_t5_bucket(dist))
    win = jnp.stack([jnp.where(from_prev[:, :, None], NEG, win), win])
    n_idx = jnp.arange(nb, dtype=jnp.int32)[:, None, None]
    m = jnp.arange(N_META, dtype=jnp.int32)[None, None, :]
    dist_m = N_META + n_idx * BLK + j[None] - m
    meta = lookup(_t5_bucket(dist_m))
    meta = jnp.pad(meta, ((0, 0), (0, 0), (0, LANES - N_META), (0, 0)), constant_values=NEG)
    return win.transpose(0, 3, 1, 2), meta.transpose(0, 3, 1, 2)


def _dup_heads(w):
    w = w.reshape(w.shape[0], N_KV_HEADS, 1, HEAD_DIM)
    return jnp.broadcast_to(w, (w.shape[0], N_KV_HEADS, 2, HEAD_DIM)).reshape(w.shape[0], KV_DUP)


def _pad_lanes(v, value=0.0):
    v = v.reshape(1, -1).astype(F32)
    return jnp.pad(v, ((0, 0), (0, LANES - v.shape[1])), constant_values=value)


def _head_rows(v):
    return jnp.broadcast_to(v.astype(F32)[:, None], (N_SSM_HEADS, BLK))


def _row_tile(rows, target):
    tm = min(rows, target)
    assert rows % tm == 0
    return tm


def kernel(x, meta_tokens, rel_bias, norm_mix_g, w_in, attn_sinks, conv_w, conv_b, dt_bias, a_log,
           d_skip, ssm_norm_g, w_attn_br, w_ssm_br, w_out, norm_ffn_g, w_router, b_router,
           w_exp_in, b_exp_in, w_exp_out, b_exp_out, norm_final_g):
    bsz, seq, _ = x.shape
    assert seq % BLK == 0 and w_in.shape[0] == 1
    nb = seq // BLK
    rows = bsz * seq
    x2d = x.reshape(rows, D_MODEL)

    wi = w_in[0]
    o_k, o_v, o_z = Q_DIM, Q_DIM + KV_DIM, Q_DIM + 2 * KV_DIM
    o_x, o_dt = o_z + D_INNER, o_z + D_INNER + CONV_DIM
    o_g = o_dt + N_SSM_HEADS
    w_proj = jnp.concatenate([
        wi[:, :o_k] * (HEAD_DIM ** -0.5 * LOG2E), _dup_heads(wi[:, o_k:o_v]),
        _dup_heads(wi[:, o_v:o_z]),
        wi[:, o_z:o_x], wi[:, o_x:o_dt], wi[:, o_g:]], axis=1).astype(BF16)
    w_dt = jnp.pad(wi[:, o_dt:o_g], ((0, 0), (0, LANES - N_SSM_HEADS))).astype(BF16)
    g_mix = norm_mix_g[0].reshape(1, D_MODEL)

    tm = _row_tile(rows, ROW_TILE)
    q, k, v, z, xbc, gate, dtraw = _inproj(x2d, g_mix, w_proj, w_dt, tm)
    _, k_m, v_m, z_m, xbc_m, _, dtraw_m = _inproj(meta_tokens.astype(F32), g_mix, w_proj, w_dt,
                                                  N_META)

    pad_meta = ((BLK - N_META, 0), (0, 0))
    pad_meta_tail = ((0, BLK - N_META), (0, 0))
    bias_win, bias_meta = _bias_tables(rel_bias, nb)
    y_attn = _attention(q, k, v, jnp.pad(k_m, pad_meta_tail), jnp.pad(v_m, pad_meta_tail),
                        bias_win, bias_meta, attn_sinks[0].astype(F32) * LOG2E, bsz, nb)

    ssd_params = (conv_w[0].astype(F32), conv_b[0].reshape(1, CONV_DIM).astype(F32),
                  _head_rows(dt_bias[0]), _head_rows(a_log[0]),
                  jnp.repeat(d_skip[0].astype(F32), SSM_HEAD_DIM).reshape(1, D_INNER),
                  ssm_norm_g[0].reshape(1, D_INNER).astype(F32))
    xbc_mp = jnp.pad(xbc_m, pad_meta)
    zero_state = jnp.zeros((N_SSM_PAIRS, D_STATE, LANES), F32)
    _, h_meta = _ssd(xbc_mp, jnp.pad(z_m, pad_meta), jnp.pad(dtraw_m, pad_meta),
                     jnp.zeros((TAIL, CONV_DIM), BF16), zero_state, *ssd_params, 1, 1,
                     BLK - N_META)
    y_ssm, _ = _ssd(xbc, z, dtraw, xbc_mp[BLK - TAIL:], h_meta[0], *ssd_params, bsz, nb, 0)

    w_r = jnp.pad(w_router[0], ((0, 0), (0, LANES - N_EXPERTS))).astype(BF16)
    b_r = _pad_lanes(b_router[0], NEG)
    h1, u2, top_idx, top_w, counts8 = _merge(
        y_attn, y_ssm, gate, x2d, w_attn_br[0].astype(BF16), w_ssm_br[0].astype(BF16),
        w_out[0].astype(BF16), norm_ffn_g[0].reshape(1, D_MODEL), w_r, b_r, tm)

    tm_e = EXPERT_TILE
    n_tiles = -(-(rows * TOP_K + N_EXPERTS * (tm_e - 1)) // tm_e)
    dest_t, info = _route(top_idx, counts8, tm_e, _row_tile(rows, ROUTE_TILE))
    counts, pstart, pend = info[0, :N_EXPERTS], info[1, :N_EXPERTS], info[2, :N_EXPERTS]
    tile_start = jnp.arange(n_tiles, dtype=jnp.int32) * tm_e
    tile_e = jnp.minimum(jnp.sum(tile_start[:, None] >= pend[None, :], axis=1),
                         N_EXPERTS - 1).astype(jnp.int32)
    n_valid = pend[-1:] // tm_e
    n_rows = n_tiles * tm_e
    n_assign = rows * TOP_K
    slot = jnp.arange(tm_e, dtype=jnp.int32)[None, :]
    pad_key = (pstart + counts)[:, None] + slot
    pad_key = jnp.where(pad_key < pend[:, None], pad_key, n_rows).reshape(-1)
    n_fill = max(pad_key.shape[0], n_rows - n_assign)
    pad_key = jnp.pad(pad_key, (0, n_fill - pad_key.shape[0]), constant_values=n_rows)
    keys = jnp.concatenate([dest_t[:TOP_K].reshape(-1), pad_key])
    toks = jnp.concatenate([jnp.tile(jnp.arange(rows, dtype=jnp.int32), TOP_K),
                            jnp.arange(n_fill, dtype=jnp.int32) % rows])
    row_tok = lax.sort((keys, toks), num_keys=1)[1][:n_rows]

    b1 = b_exp_in[0].astype(F32).reshape(N_EXPERTS, 2 * D_FF // FF_CHUNK, LANES, 2)
    b1 = b1.transpose(0, 1, 3, 2).reshape(N_EXPERTS, 1, 2 * D_FF)
    b2 = b_exp_out[0].reshape(N_EXPERTS, 1, D_MODEL).astype(F32)
    y_rows = None
    bounds = [n_tiles * p // MOE_SPLITS[-1] for p in MOE_SPLITS]
    assert all(t1 > t0 for t0, t1 in zip(bounds[:-1], bounds[1:]))
    for t0, t1 in zip(bounds[:-1], bounds[1:]):
        y_rows = _moe(tile_e[t0:t1], jnp.clip(n_valid - t0, 0, t1 - t0),
                      u2[row_tok[t0 * tm_e:t1 * tm_e]], w_exp_in[0], b1, w_exp_out[0], b2, tm_e,
                      t0, n_tiles, y_rows)

    yg = [y_rows[dest_t[k]] for k in range(TOP_K)]
    out = _final(h1, yg, top_w, norm_final_g.reshape(1, D_MODEL).astype(F32), tm)
    return out.reshape(bsz, seq, D_MODEL)
```
